```python
import jax, jax.numpy as jnp
from jax import lax
import numpy as np

D_MODEL = 1024
BATCH = 8
SEQ = 8192
DEPTH = 4

CHUNK = 64
N_MIXERS = 4
EPS = 1e-6
D_FF = 2816
NEG_INF = -1e30

HGRN_HEADS = 8
HGRN_HEAD_DIM = D_MODEL // HGRN_HEADS

GLA_HEADS = 4
GLA_KEY_DIM = D_MODEL // 2
GLA_VALUE_DIM = D_MODEL
GLA_HEAD_K = GLA_KEY_DIM // GLA_HEADS
GLA_HEAD_V = GLA_VALUE_DIM // GLA_HEADS
GLA_GATE_RANK = 16
GLA_GATE_NORMALIZER = 16.0

ATTN_HEADS = 16
ATTN_HEAD_DIM = D_MODEL // ATTN_HEADS
LEFT_CHUNKS = 8
BAND = (LEFT_CHUNKS + 1) * CHUNK
REL_CLIP = 2 * CHUNK
N_REL = REL_CLIP + CHUNK

LRU_WIDTH = D_MODEL
LRU_HEADS = 4
LRU_BLOCK = LRU_WIDTH // LRU_HEADS
CONV_WIDTH = 4
LRU_C = 8.0

kernel_name = 'hybrid_chunk_causal_encoder'


def rms_norm(x, gain):
    xf = x.astype(jnp.float32)
    y = xf * lax.rsqrt(jnp.mean(xf * xf, axis=-1, keepdims=True) + EPS)
    return (y * gain.astype(jnp.float32)).astype(x.dtype)


def swiglu(h, w_gate_up, w_down):
    g, u = jnp.split(h @ w_gate_up, 2, axis=-1)
    return (jax.nn.silu(g) * u) @ w_down


def split_heads(t, n_heads):
    b, l, _ = t.shape
    return t.reshape(b, l, n_heads, -1).transpose(0, 2, 1, 3)


def merge_heads(t):
    b, h, l, d = t.shape
    return t.transpose(0, 2, 1, 3).reshape(b, l, h * d)


def chunk_gated_linear_attention(q, k, v, log_f, scale):
    b, h, l, dk = q.shape
    dv = v.shape[-1]
    n = l // CHUNK
    out_dtype = v.dtype

    def to_chunks(t):
        t = t.astype(jnp.float32)
        return jnp.moveaxis(t.reshape(b, h, n, CHUNK, t.shape[-1]), 2, 0)

    qc, kc, vc, gc = to_chunks(q * scale), to_chunks(k), to_chunks(v), to_chunks(log_f)
    causal = jnp.tril(jnp.ones((CHUNK, CHUNK), dtype=bool))

    def step(state, inp):
        q_i, k_i, v_i, g_i = inp
        cum = jnp.cumsum(g_i, axis=-2)
        rel = cum[..., :, None, :] - cum[..., None, :, :]
        decay = jnp.exp(jnp.where(causal[:, :, None], rel, -jnp.inf))
        scores = jnp.einsum('bhtk,bhsk,bhtsk->bhts', q_i, k_i, decay)
        o = (jnp.einsum('bhts,bhsv->bhtv', scores, v_i)
             + jnp.einsum('bhtk,bhkv->bhtv', q_i * jnp.exp(cum), state))
        last = cum[..., -1:, :]
        state = (jnp.exp(last[..., 0, :])[..., None] * state
                 + jnp.einsum('bhsk,bhsv->bhkv', k_i * jnp.exp(last - cum), v_i))
        return state, o

    s0 = jnp.zeros((b, h, dk, dv), jnp.float32)
    _, o = lax.scan(step, s0, (qc, kc, vc, gc))
    return jnp.moveaxis(o, 0, 2).reshape(b, h, l, dv).astype(out_dtype)


def hgrn2_mixer(h, w_in, lower_bound, norm_gain, w_out):
    q, f, i, g = jnp.split(h @ w_in, 4, axis=-1)
    q = jax.nn.silu(q)
    fgate = lower_bound + (1.0 - lower_bound) * jax.nn.sigmoid(f.astype(jnp.float32))
    k = 1.0 - fgate
    log_f = jnp.log(fgate)
    o = chunk_gated_linear_attention(split_heads(q, HGRN_HEADS), split_heads(k, HGRN_HEADS),
                                     split_heads(i, HGRN_HEADS), split_heads(log_f, HGRN_HEADS),
                                     HGRN_HEAD_DIM ** -0.5)
    o = merge_heads(rms_norm(o, norm_gain)) * jax.nn.silu(g)
    return o @ w_out


def gla_mixer(h, w_in, w_gate_down, w_gate_up, gate_bias, norm_gain, w_out):
    q, k, v, g = jnp.split(h @ w_in, [GLA_KEY_DIM, 2 * GLA_KEY_DIM, 2 * GLA_KEY_DIM + GLA_VALUE_DIM], axis=-1)
    gate_logits = ((h @ w_gate_down) @ w_gate_up + gate_bias).astype(jnp.float32)
    log_f = jax.nn.log_sigmoid(gate_logits) / GLA_GATE_NORMALIZER
    o = chunk_gated_linear_attention(split_heads(q, GLA_HEADS), split_heads(k, GLA_HEADS),
                                     split_heads(v, GLA_HEADS), split_heads(log_f, GLA_HEADS),
                                     GLA_HEAD_K ** -0.5)
    o = merge_heads(rms_norm(o, norm_gain)) * jax.nn.silu(g)
    return o @ w_out


def chunked_relbias_attention(h, w_qkv, rel_table, w_out):
    b, l, _ = h.shape
    n = l // CHUNK
    pad = BAND - CHUNK
    q, k, v = jnp.split(h @ w_qkv, 3, axis=-1)
    q = split_heads(q * ATTN_HEAD_DIM ** -0.5, ATTN_HEADS)
    k = split_heads(k, ATTN_HEADS)
    v = split_heads(v, ATTN_HEADS)
    kp = jnp.pad(k, ((0, 0), (0, 0), (pad, 0), (0, 0)))
    vp = jnp.pad(v, ((0, 0), (0, 0), (pad, 0), (0, 0)))
    dist = jnp.arange(CHUNK)[:, None] + pad - jnp.arange(BAND)[None, :]
    rel_idx = jnp.clip(dist, -(CHUNK - 1), REL_CLIP) + (CHUNK - 1)
    bias = rel_table.astype(jnp.float32)[:, rel_idx]
    qc = jnp.moveaxis(q.reshape(b, ATTN_HEADS, n, CHUNK, ATTN_HEAD_DIM), 2, 0)

    def one_chunk(args):
        c, q_c = args
        start = c * CHUNK
        k_band = lax.dynamic_slice_in_dim(kp, start, BAND, axis=2)
        v_band = lax.dynamic_slice_in_dim(vp, start, BAND, axis=2)
        scores = jnp.einsum('bhtd,bhsd->bhts', q_c, k_band).astype(jnp.float32) + bias
        key_pos = start - pad + jnp.arange(BAND)
        scores = jnp.where(key_pos >= 0, scores, NEG_INF)
        p = jax.nn.softmax(scores, axis=-1).astype(v_band.dtype)
        return jnp.einsum('bhts,bhsd->bhtd', p, v_band)

    o = lax.map(one_chunk, (jnp.arange(n), qc))
    o = jnp.moveaxis(o, 0, 2).reshape(b, ATTN_HEADS, l, ATTN_HEAD_DIM)
    return merge_heads(o) @ w_out


def causal_depthwise_conv(x, w, bias):
    y = lax.conv_general_dilated(x, w[:, None, :].astype(x.dtype), window_strides=(1,),
                                 padding=[(CONV_WIDTH - 1, 0)],
                                 dimension_numbers=('NWC', 'WIO', 'NWC'),
                                 feature_group_count=x.shape[-1])
    return y + bias


def _linear_combine(left, right):
    a1, b1 = left
    a2, b2 = right
    return a1 * a2, a2 * b1 + b2


def rg_lru(x, w_a, b_a, w_x, b_x, lam):
    b, l, w = x.shape
    xf = x.astype(jnp.float32)
    xh = xf.reshape(b, l, LRU_HEADS, LRU_BLOCK)
    r = jax.nn.sigmoid(jnp.einsum('blhi,hij->blhj', xh, w_a.astype(jnp.float32)).reshape(b, l, w) + b_a)
    i = jax.nn.sigmoid(jnp.einsum('blhi,hij->blhj', xh, w_x.astype(jnp.float32)).reshape(b, l, w) + b_x)
    log_a = -LRU_C * r * jax.nn.softplus(-lam.astype(jnp.float32))
    a = jnp.exp(log_a)
    mult = jnp.sqrt(-jnp.expm1(2.0 * log_a))
    mult = mult.at[:, 0].set(1.0)
    _, hs = lax.associative_scan(_linear_combine, (a, mult * i * xf), axis=1)
    return hs.astype(x.dtype)


def rglru_mixer(h, w_in, conv_w, conv_b, w_a, b_a, w_x, b_x, lam, w_out):
    y, xb = jnp.split(h @ w_in, 2, axis=-1)
    y = jax.nn.gelu(y, approximate=True)
    xb = causal_depthwise_conv(xb, conv_w, conv_b)
    hr = rg_lru(xb, w_a, b_a, w_x, b_x, lam)
    return (y * hr) @ w_out


def setup_inputs(seed: int = 0) -> dict:
    key = jax.random.key(seed)
    keys = iter(jax.random.split(key, 32))

    def normal(shape, scale):
        return scale * jax.random.normal(next(keys), shape, jnp.float32)

    n_a, n_b, n_c, n_d = [len(range(m, DEPTH, N_MIXERS)) for m in range(N_MIXERS)]
    d = D_MODEL
    a0 = jax.random.uniform(next(keys), (n_d, LRU_WIDTH), jnp.float32, minval=0.9, maxval=0.999)
    return {
        'x': normal((BATCH, SEQ, d), 1.0),
        'ffn_w_gate_up': normal((DEPTH, 2, d, 2 * D_FF), d ** -0.5),
        'ffn_w_down': normal((DEPTH, 2, D_FF, d), D_FF ** -0.5),
        'norm_gains': 1.0 + normal((DEPTH, 3, d), 0.02),
        'final_norm_gain': 1.0 + normal((d,), 0.02),
        'hgrn_w_in': normal((n_a, d, 4 * d), d ** -0.5),
        'hgrn_lower_bound_logits': normal((DEPTH + 1, d), 0.5),
        'hgrn_norm_gain': 1.0 + normal((n_a, HGRN_HEAD_DIM), 0.02),
        'hgrn_w_out': normal((n_a, d, d), d ** -0.5),
        'gla_w_in': normal((n_b, d, 2 * GLA_KEY_DIM + 2 * GLA_VALUE_DIM), d ** -0.5),
        'gla_w_gate_down': normal((n_b, d, GLA_GATE_RANK), d ** -0.5),
        'gla_w_gate_up': normal((n_b, GLA_GATE_RANK, GLA_KEY_DIM), GLA_GATE_RANK ** -0.5),
        'gla_gate_bias': normal((n_b, GLA_KEY_DIM), 0.1),
        'gla_norm_gain': 1.0 + normal((n_b, GLA_HEAD_V), 0.02),
        'gla_w_out': normal((n_b, GLA_VALUE_DIM, d), GLA_VALUE_DIM ** -0.5),
        'attn_w_qkv': normal((n_c, d, 3 * d), d ** -0.5),
        'attn_rel_bias': normal((n_c, ATTN_HEADS, N_REL), 0.1),
        'attn_w_out': normal((n_c, d, d), d ** -0.5),
        'lru_w_in': normal((n_d, d, 2 * LRU_WIDTH), d ** -0.5),
        'lru_conv_w': normal((n_d, CONV_WIDTH, LRU_WIDTH), CONV_WIDTH ** -0.5),
        'lru_conv_b': normal((n_d, LRU_WIDTH), 0.02),
        'lru_w_a': normal((n_d, LRU_HEADS, LRU_BLOCK, LRU_BLOCK), LRU_BLOCK ** -0.5),
        'lru_b_a': normal((n_d, LRU_WIDTH), 0.02),
        'lru_w_x': normal((n_d, LRU_HEADS, LRU_BLOCK, LRU_BLOCK), LRU_BLOCK ** -0.5),
        'lru_b_x': normal((n_d, LRU_WIDTH), 0.02),
        'lru_lambda': jnp.log(a0) - jnp.log1p(-a0),
        'lru_w_out': normal((n_d, LRU_WIDTH, d), LRU_WIDTH ** -0.5),
    }


def reference(x, ffn_w_gate_up, ffn_w_down, norm_gains, final_norm_gain,
              hgrn_w_in, hgrn_lower_bound_logits, hgrn_norm_gain, hgrn_w_out,
              gla_w_in, gla_w_gate_down, gla_w_gate_up, gla_gate_bias, gla_norm_gain, gla_w_out,
              attn_w_qkv, attn_rel_bias, attn_w_out,
              lru_w_in, lru_conv_w, lru_conv_b, lru_w_a, lru_b_a, lru_w_x, lru_b_x, lru_lambda, lru_w_out):
    lower_bounds = jnp.cumsum(jax.nn.softmax(hgrn_lower_bound_logits.astype(jnp.float32), axis=0), axis=0)
    for i in range(DEPTH):
        m, j = i % N_MIXERS, i // N_MIXERS
        x = x + 0.5 * swiglu(rms_norm(x, norm_gains[i, 0]), ffn_w_gate_up[i, 0], ffn_w_down[i, 0])
        h = rms_norm(x, norm_gains[i, 1])
        if m == 0:
            y = hgrn2_mixer(h, hgrn_w_in[j], lower_bounds[i], hgrn_norm_gain[j], hgrn_w_out[j])
        elif m == 1:
            y = gla_mixer(h, gla_w_in[j], gla_w_gate_down[j], gla_w_gate_up[j], gla_gate_bias[j],
                          gla_norm_gain[j], gla_w_out[j])
        elif m == 2:
            y = chunked_relbias_attention(h, attn_w_qkv[j], attn_rel_bias[j], attn_w_out[j])
        else:
            y = rglru_mixer(h, lru_w_in[j], lru_conv_w[j], lru_conv_b[j], lru_w_a[j], lru_b_a[j],
                            lru_w_x[j], lru_b_x[j], lru_lambda[j], lru_w_out[j])
        x = x + y
        x = x + 0.5 * swiglu(rms_norm(x, norm_gains[i, 2]), ffn_w_gate_up[i, 1], ffn_w_down[i, 1])
    return rms_norm(x, final_norm_gain)
```

```python
import functools
import math

import numpy as np
import jax
import jax.numpy as jnp
from jax import lax
from jax.experimental import pallas as pl
from jax.experimental.pallas import tpu as pltpu

F32 = jnp.float32
BF16 = jnp.bfloat16

EPS = 1e-6
CHUNK = 64
N_MIXERS = 4
NEG_INF = -1e30
HGRN_HEADS = 8
GLA_HEADS = 4
GLA_GATE_NORMALIZER = 16.0
ATTN_HEADS = 16
LEFT_CHUNKS = 8
REL_CLIP = 2 * CHUNK
LRU_HEADS = 4
CONV_WIDTH = 4
LRU_C = 8.0

LANES = 128
VMEM_LIMIT_BYTES = 56 * 1024 * 1024

FFN_ROWS = 512
FFN_COLS = 1408
GLA_CHUNK = 64
GLA_ROWS = 256
ATTN_ROWS = 512
LRU_ROWS = 256


def _const_spec(shape):
    nd = len(shape)
    return pl.BlockSpec(shape, lambda *_: (0,) * nd, pipeline_mode=pl.Buffered(1))


def _rms(x, gain):
    ms = jnp.mean(x * x, axis=-1, keepdims=True)
    return x * lax.rsqrt(ms + EPS) * gain


def _silu(x):
    return x * jax.nn.sigmoid(x)


def _dot(a, b):
    return jnp.dot(a, b, preferred_element_type=F32)


def _dot_nt(a, b):
    return lax.dot_general(a, b, (((1,), (1,)), ((), ())), preferred_element_type=F32)


def _dot_tn(a, b):
    return lax.dot_general(a, b, (((0,), (0,)), ((), ())), preferred_element_type=F32)


def _ffn_kernel(x_ref, gain_ref, wgu_ref, wd_ref, fgain_ref, o_ref, *, d_ff, cols, final_norm):
    x = x_ref[...]
    h = _rms(x, gain_ref[...]).astype(BF16)
    acc = None
    for f in range(d_ff // cols):
        g = _dot(h, wgu_ref[:, f * cols:(f + 1) * cols])
        u = _dot(h, wgu_ref[:, d_ff + f * cols:d_ff + (f + 1) * cols])
        a = (_silu(g) * u).astype(BF16)
        part = _dot(a, wd_ref[f * cols:(f + 1) * cols, :])
        acc = part if acc is None else acc + part
    y = x + 0.5 * acc
    if final_norm:
        y = _rms(y, fgain_ref[...])
    o_ref[...] = y


def _ffn(x2, gain, w_gate_up, w_down, final_gain, final_norm):
    t, d = x2.shape
    d_ff = w_down.shape[0]
    rows = min(FFN_ROWS, t)
    cols = FFN_COLS if d_ff % FFN_COLS == 0 else d_ff
    assert t % rows == 0
    kern = functools.partial(_ffn_kernel, d_ff=d_ff, cols=cols, final_norm=final_norm)
    return pl.pallas_call(
        kern,
        grid=(t // rows,),
        in_specs=[
            pl.BlockSpec((rows, d), lambda i: (i, 0)),
            _const_spec((1, d)),
            _const_spec((d, 2 * d_ff)),
            _const_spec((d_ff, d)),
            _const_spec((1, d)),
        ],
        out_specs=pl.BlockSpec((rows, d), lambda i: (i, 0)),
        out_shape=jax.ShapeDtypeStruct((t, d), F32),
        compiler_params=pltpu.CompilerParams(
            dimension_semantics=("parallel",), vmem_limit_bytes=VMEM_LIMIT_BYTES),
        name="ffn",
    )(x2, gain.reshape(1, d), w_gate_up.astype(BF16), w_down.astype(BF16), final_gain.reshape(1, d))


def _gla_level_count(chunk):
    return int(math.log2(chunk))


def _gla_decay_matrices(chunk):
    levels = _gla_level_count(chunk)
    t = np.arange(chunk)[:, None]
    j = np.arange(chunk)[None, :]
    mats = []
    for lvl in range(levels):
        half = 1 << lvl
        ref = (t // (2 * half)) * (2 * half) + half - 1
        upper = t > ref
        mats.append(np.where(upper, (j > ref) & (j <= t), (j > t) & (j <= ref)))
    mats.append(j <= t)
    mats.append(j > t)
    d = np.stack(mats).astype(np.float32)
    return np.concatenate([d, d, d], axis=-1)


def _split3(g):
    g1 = g.astype(BF16)
    r1 = g - g1.astype(F32)
    g2 = r1.astype(BF16)
    g3 = (r1 - g2.astype(F32)).astype(BF16)
    return jnp.concatenate([g1, g2, g3], axis=0)


def _gla_chunk_loop(q_ref, k_ref, v_ref, g_ref, og_ref, dmat_ref, ngain_ref, st_ref, y_ref,
                    *, rows, heads, dk, dv):
    c = GLA_CHUNK
    levels = _gla_level_count(c)
    ti = lax.broadcasted_iota(jnp.int32, (c, c), 0)
    si = lax.broadcasted_iota(jnp.int32, (c, c), 1)
    diag_mask = ti == si
    level_masks = []
    for lvl in range(levels):
        same = (ti >> (lvl + 1)) == (si >> (lvl + 1))
        level_masks.append(same & (((ti >> lvl) & 1) == 1) & (((si >> lvl) & 1) == 0))
    ngain = ngain_ref[...]

    def body(ci, carry):
        r0 = pl.multiple_of(ci * c, c)
        rs = pl.ds(r0, c)
        g3 = _split3(g_ref[rs, :])
        decay = [jnp.exp(_dot(dmat_ref[l], g3)) for l in range(levels + 2)]
        e_cum, e_rev = decay[levels], decay[levels + 1]
        q = q_ref[rs, :]
        k = k_ref[rs, :]
        for h in range(heads):
            ks = slice(h * dk, (h + 1) * dk)
            vs = slice(h * dv, (h + 1) * dv)
            qh, kh = q[:, ks], k[:, ks]
            vh = v_ref[rs, vs]
            a = jnp.where(diag_mask, _dot_nt(qh.astype(BF16), kh.astype(BF16)), 0.0)
            for lvl in range(levels):
                e = decay[lvl][:, ks]
                s = _dot_nt((qh * e).astype(BF16), (kh * e).astype(BF16))
                a = a + jnp.where(level_masks[lvl], s, 0.0)
            st = st_ref[h]
            o = _dot(a.astype(BF16), vh) + _dot_nt((qh * e_cum[:, ks]).astype(BF16), st.astype(BF16))
            e_last = e_cum[c - 1:c, ks]
            st_ref[h] = st * e_last + _dot_tn(vh, (kh * e_rev[:, ks]).astype(BF16))
            o = _rms(o, ngain) * _silu(og_ref[rs, vs])
            y_ref[rs, vs] = o.astype(BF16)
        return carry

    lax.fori_loop(0, rows // c, body, 0)


def _hgrn_kernel(x_ref, gain_ref, win_ref, lb_ref, dmat_ref, ngain_ref, wout_ref, o_ref,
                 q_s, k_s, g_s, v_s, og_s, y_s, st_s, *, rows, heads, dk):
    d = heads * dk

    @pl.when(pl.program_id(1) == 0)
    def _():
        st_s[...] = jnp.zeros_like(st_s)

    x = x_ref[...]
    h = _rms(x, gain_ref[...]).astype(BF16)
    q_s[...] = _silu(_dot(h, win_ref[:, 0:d])) * (dk ** -0.5)
    lb = lb_ref[...]
    fgate = lb + (1.0 - lb) * jax.nn.sigmoid(_dot(h, win_ref[:, d:2 * d]))
    k_s[...] = 1.0 - fgate
    g_s[...] = jnp.log(fgate)
    v_s[...] = _dot(h, win_ref[:, 2 * d:3 * d]).astype(BF16)
    og_s[...] = _dot(h, win_ref[:, 3 * d:4 * d])
    _gla_chunk_loop(q_s, k_s, v_s, g_s, og_s, dmat_ref, ngain_ref, st_s, y_s,
                    rows=rows, heads=heads, dk=dk, dv=dk)
    o_ref[...] = x + _dot(y_s[...], wout_ref[...])


def _seq_call(kern, x, consts, scratch, rows, name):
    b, l, d = x.shape
    assert l % rows == 0
    xspec = pl.BlockSpec((None, rows, d), lambda bi, ti: (bi, ti, 0))
    return pl.pallas_call(
        kern,
        grid=(b, l // rows),
        in_specs=[xspec] + [_const_spec(c.shape) for c in consts],
        out_specs=xspec,
        out_shape=jax.ShapeDtypeStruct(x.shape, F32),
        scratch_shapes=scratch,
        compiler_params=pltpu.CompilerParams(
            dimension_semantics=("arbitrary", "arbitrary"), vmem_limit_bytes=VMEM_LIMIT_BYTES),
        name=name,
    )(x, *consts)


def _hgrn_layer(x, gain, w_in, lower_bound, norm_gain, w_out):
    b, l, d = x.shape
    heads, dk = HGRN_HEADS, d // HGRN_HEADS
    rows = min(GLA_ROWS, l)
    dmat = jnp.asarray(_gla_decay_matrices(GLA_CHUNK), BF16)
    consts = [gain.reshape(1, d), w_in.astype(BF16), lower_bound.reshape(1, d), dmat,
              norm_gain.reshape(1, dk), w_out.astype(BF16)]
    scratch = [pltpu.VMEM((rows, d), F32), pltpu.VMEM((rows, d), F32), pltpu.VMEM((rows, d), F32),
               pltpu.VMEM((rows, d), BF16), pltpu.VMEM((rows, d), F32), pltpu.VMEM((rows, d), BF16),
               pltpu.VMEM((heads, dk, dk), F32)]
    kern = functools.partial(_hgrn_kernel, rows=rows, heads=heads, dk=dk)
    return _seq_call(kern, x, consts, scratch, rows, "hgrn_mixer")


def _log_sigmoid(x):
    return jnp.minimum(x, 0.0) - jnp.log(1.0 + jnp.exp(-jnp.abs(x)))


def _gla_kernel(x_ref, gain_ref, win_ref, wgd_ref, wgu_ref, gbias_ref, dmat_ref, ngain_ref, wout_ref,
                o_ref, q_s, k_s, g_s, v_s, og_s, y_s, st_s, *, rows, heads, dk, dv):
    kd, vd = heads * dk, heads * dv

    @pl.when(pl.program_id(1) == 0)
    def _():
        st_s[...] = jnp.zeros_like(st_s)

    x = x_ref[...]
    h = _rms(x, gain_ref[...]).astype(BF16)
    q_s[...] = _dot(h, win_ref[:, 0:kd]) * (dk ** -0.5)
    k_s[...] = _dot(h, win_ref[:, kd:2 * kd])
    v_s[...] = _dot(h, win_ref[:, 2 * kd:2 * kd + vd]).astype(BF16)
    og_s[...] = _dot(h, win_ref[:, 2 * kd + vd:2 * kd + 2 * vd])
    low = _dot(h, wgd_ref[...]).astype(BF16)
    logits = _dot(low, wgu_ref[...]) + gbias_ref[...]
    g_s[...] = _log_sigmoid(logits) * (1.0 / GLA_GATE_NORMALIZER)
    _gla_chunk_loop(q_s, k_s, v_s, g_s, og_s, dmat_ref, ngain_ref, st_s, y_s,
                    rows=rows, heads=heads, dk=dk, dv=dv)
    o_ref[...] = x + _dot(y_s[...], wout_ref[...])


def _gla_layer(x, gain, w_in, w_gate_down, w_gate_up, gate_bias, norm_gain, w_out):
    b, l, d = x.shape
    kd = gate_bias.shape[0]
    vd = w_out.shape[0]
    heads = GLA_HEADS
    dk, dv = kd // heads, vd // heads
    rank = w_gate_down.shape[1]
    rank_pad = -(-rank // LANES) * LANES
    wgd = jnp.pad(w_gate_down, ((0, 0), (0, rank_pad - rank))).astype(BF16)
    wgu = jnp.pad(w_gate_up, ((0, rank_pad - rank), (0, 0))).astype(BF16)
    rows = min(GLA_ROWS, l)
    dmat = jnp.asarray(_gla_decay_matrices(GLA_CHUNK), BF16)
    consts = [gain.reshape(1, d), w_in.astype(BF16), wgd, wgu, gate_bias.reshape(1, kd), dmat,
              norm_gain.reshape(1, dv), w_out.astype(BF16)]
    scratch = [pltpu.VMEM((rows, kd), F32), pltpu.VMEM((rows, kd), F32), pltpu.VMEM((rows, kd), F32),
               pltpu.VMEM((rows, vd), BF16), pltpu.VMEM((rows, vd), F32), pltpu.VMEM((rows, vd), BF16),
               pltpu.VMEM((heads, dv, dk), F32)]
    kern = functools.partial(_gla_kernel, rows=rows, heads=heads, dk=dk, dv=dv)
    return _seq_call(kern, x, consts, scratch, rows, "gla_mixer")


def _attn_kernel(x_ref, gain_ref, wqkv_ref, bias_ref, wout_ref, o_ref, q_s, k_s, v_s, y_s,
                 *, rows, hist, d, dh):
    tb = pl.program_id(1)
    pair = 2 * CHUNK
    band = hist + pair
    npairs = d // LANES

    @pl.when(tb == 0)
    def _():
        k_s[0:hist, :] = jnp.zeros((hist, d), BF16)
        v_s[0:hist, :] = jnp.zeros((hist, d), BF16)

    x = x_ref[...]
    h = _rms(x, gain_ref[...]).astype(BF16)
    q_s[...] = (_dot(h, wqkv_ref[:, 0:d]) * (dh ** -0.5)).astype(BF16)
    k_s[hist:hist + rows, :] = _dot(h, wqkv_ref[:, d:2 * d]).astype(BF16)
    v_s[hist:hist + rows, :] = _dot(h, wqkv_ref[:, 2 * d:3 * d]).astype(BF16)

    lane = lax.broadcasted_iota(jnp.int32, (pair, LANES), 1)
    first_head = lane < dh
    col = lax.broadcasted_iota(jnp.int32, (2 * pair, band), 1)

    def body(p, carry):
        r0 = pl.multiple_of(p * pair, pair)
        key_pos = col + (tb * rows + r0 - hist)
        valid = key_pos >= 0
        for hp in range(npairs):
            ls = slice(hp * LANES, (hp + 1) * LANES)
            q2 = q_s[pl.ds(r0, pair), ls]
            zero = jnp.zeros_like(q2)
            qq = jnp.concatenate([jnp.where(first_head, q2, zero), jnp.where(first_head, zero, q2)], axis=0)
            kb = k_s[pl.ds(r0, band), ls]
            vb = v_s[pl.ds(r0, band), ls]
            s = _dot_nt(qq, kb) + bias_ref[hp]
            s = jnp.where(valid, s, NEG_INF)
            m = jnp.max(s, axis=-1, keepdims=True)
            e = jnp.exp(s - m)
            den = jnp.sum(e, axis=-1, keepdims=True)
            o2 = _dot(e.astype(BF16), vb) / den
            y_s[pl.ds(r0, pair), ls] = jnp.where(first_head, o2[0:pair], o2[pair:2 * pair]).astype(BF16)
        return carry

    lax.fori_loop(0, rows // pair, body, 0)
    o_ref[...] = x + _dot(y_s[...], wout_ref[...])
    k_s[0:hist, :] = k_s[rows:rows + hist, :]
    v_s[0:hist, :] = v_s[rows:rows + hist, :]


def _attn_bias(rel_table):
    heads = rel_table.shape[0]
    hist = LEFT_CHUNKS * CHUNK
    band1 = hist + CHUNK
    dist = jnp.arange(CHUNK)[:, None] + hist - jnp.arange(band1)[None, :]
    rel_idx = jnp.clip(dist, -(CHUNK - 1), REL_CLIP) + (CHUNK - 1)
    bias = rel_table.astype(F32)[:, rel_idx]
    neg = jnp.full((heads, CHUNK, CHUNK), NEG_INF, F32)
    both = jnp.concatenate([jnp.concatenate([bias, neg], axis=-1),
                            jnp.concatenate([neg, bias], axis=-1)], axis=1)
    return both.reshape(heads // 2, 4 * CHUNK, band1 + CHUNK)


def _attn_layer(x, gain, w_qkv, rel_table, w_out):
    b, l, d = x.shape
    dh = d // ATTN_HEADS
    hist = LEFT_CHUNKS * CHUNK
    rows = min(ATTN_ROWS, l)
    assert 2 * dh == LANES and rows % (2 * CHUNK) == 0 and rows >= hist
    consts = [gain.reshape(1, d), w_qkv.astype(BF16), _attn_bias(rel_table), w_out.astype(BF16)]
    scratch = [pltpu.VMEM((rows, d), BF16), pltpu.VMEM((hist + rows, d), BF16),
               pltpu.VMEM((hist + rows, d), BF16), pltpu.VMEM((rows, d), BF16)]
    kern = functools.partial(_attn_kernel, rows=rows, hist=hist, d=d, dh=dh)
    return _seq_call(kern, x, consts, scratch, rows, "attn_mixer")


def _gelu_tanh(x):
    return 0.5 * x * (1.0 + jnp.tanh(math.sqrt(2.0 / math.pi) * (x + 0.044715 * (x * x * x))))


def _lru_kernel(x_ref, gain_ref, win_ref, convw_ref, convb_ref, wa_ref, ba_ref, wx_ref, bx_ref,
                lam_ref, wout_ref, o_ref, tail_s, h_s, *, rows, w, blocks):
    tb = pl.program_id(1)
    tail = tail_s.shape[0]

    @pl.when(tb == 0)
    def _():
        tail_s[...] = jnp.zeros_like(tail_s)
        h_s[...] = jnp.zeros_like(h_s)

    x = x_ref[...]
    h = _rms(x, gain_ref[...]).astype(BF16)
    y = _gelu_tanh(_dot(h, win_ref[:, 0:w]))
    xb = _dot(h, win_ref[:, w:2 * w])

    ext = jnp.concatenate([tail_s[...], xb], axis=0)
    tail_s[...] = xb[rows - tail:rows, :]
    convw = convw_ref[...]
    xc = xb * convw[CONV_WIDTH - 1:CONV_WIDTH, :] + convb_ref[...]
    for i in range(CONV_WIDTH - 1):
        shift = CONV_WIDTH - 1 - i
        xc = xc + pltpu.roll(ext, shift, axis=0)[tail:tail + rows, :] * convw[i:i + 1, :]

    bw = w // blocks
    xcb = xc.astype(BF16)
    ra = jnp.concatenate([_dot(xcb[:, j * bw:(j + 1) * bw], wa_ref[j]) for j in range(blocks)], axis=-1)
    ri = jnp.concatenate([_dot(xcb[:, j * bw:(j + 1) * bw], wx_ref[j]) for j in range(blocks)], axis=-1)
    r = jax.nn.sigmoid(ra + ba_ref[...])
    gi = jax.nn.sigmoid(ri + bx_ref[...])
    lam = lam_ref[...]
    softplus_neg = jnp.maximum(-lam, 0.0) + jnp.log(1.0 + jnp.exp(-jnp.abs(lam)))
    log_a = -LRU_C * r * softplus_neg
    a = jnp.exp(log_a)
    th = jnp.tanh(log_a)
    mult = jnp.sqrt(-2.0 * th / (1.0 - th))
    row = lax.broadcasted_iota(jnp.int32, (rows, w), 0)
    mult = jnp.where((row + tb * rows) == 0, 1.0, mult)
    bb = mult * gi * xc

    shift = 1
    while shift < rows:
        keep = row >= shift
        a_prev = jnp.where(keep, pltpu.roll(a, shift, axis=0), 1.0)
        b_prev = jnp.where(keep, pltpu.roll(bb, shift, axis=0), 0.0)
        bb = a * b_prev + bb
        a = a * a_prev
        shift *= 2
    hs = bb + a * h_s[...]
    h_s[...] = hs[rows - 1:rows, :]
    o_ref[...] = x + _dot((y * hs).astype(BF16), wout_ref[...])


def _lru_layer(x, gain, w_in, conv_w, conv_b, w_a, b_a, w_x, b_x, lam, w_out):
    b, l, d = x.shape
    w = w_out.shape[0]
    blocks = w_a.shape[0]
    rows = min(LRU_ROWS, l)
    row = lambda v: v.reshape(1, -1)
    consts = [row(gain), w_in.astype(BF16), conv_w, row(conv_b), w_a.astype(BF16), row(b_a),
              w_x.astype(BF16), row(b_x), row(lam), w_out.astype(BF16)]
    scratch = [pltpu.VMEM((8, w), F32), pltpu.VMEM((1, w), F32)]
    kern = functools.partial(_lru_kernel, rows=rows, w=w, blocks=blocks)
    return _seq_call(kern, x, consts, scratch, rows, "lru_mixer")


def kernel(x, ffn_w_gate_up, ffn_w_down, norm_gains, final_norm_gain, hgrn_w_in, hgrn_lower_bound_logits, hgrn_norm_gain, hgrn_w_out, gla_w_in, gla_w_gate_down, gla_w_gate_up, gla_gate_bias, gla_norm_gain, gla_w_out, attn_w_qkv, attn_rel_bias, attn_w_out, lru_w_in, lru_conv_w, lru_conv_b, lru_w_a, lru_b_a, lru_w_x, lru_b_x, lru_lambda, lru_w_out):
    b, l, d = x.shape
    depth = ffn_w_gate_up.shape[0]
    lower_bounds = jnp.cumsum(jax.nn.softmax(hgrn_lower_bound_logits.astype(F32), axis=0), axis=0)

    def ffn(xx, i, s, final):
        y = _ffn(xx.reshape(b * l, d), norm_gains[i, 2 * s], ffn_w_gate_up[i, s], ffn_w_down[i, s],
                 final_norm_gain, final)
        return y.reshape(b, l, d)

    for i in range(depth):
        m, j = i % N_MIXERS, i // N_MIXERS
        x = ffn(x, i, 0, False)
        g = norm_gains[i, 1]
        if m == 0:
            x = _hgrn_layer(x, g, hgrn_w_in[j], lower_bounds[i], hgrn_norm_gain[j], hgrn_w_out[j])
        elif m == 1:
            x = _gla_layer(x, g, gla_w_in[j], gla_w_gate_down[j], gla_w_gate_up[j], gla_gate_bias[j],
                           gla_norm_gain[j], gla_w_out[j])
        elif m == 2:
            x = _attn_layer(x, g, attn_w_qkv[j], attn_rel_bias[j], attn_w_out[j])
        else:
            x = _lru_layer(x, g, lru_w_in[j], lru_conv_w[j], lru_conv_b[j], lru_w_a[j], lru_b_a[j],
                           lru_w_x[j], lru_b_x[j], lru_lambda[j], lru_w_out[j])
        x = ffn(x, i, 1, i == depth - 1)
    return x
```

```python
import functools
import math

import numpy as np
import jax
import jax.numpy as jnp
from jax import lax
from jax.experimental import pallas as pl
from jax.experimental.pallas import tpu as pltpu

F32 = jnp.float32
BF16 = jnp.bfloat16

EPS = 1e-6
CHUNK = 64
N_MIXERS = 4
NEG_INF = -1e30
HGRN_HEADS = 8
GLA_HEADS = 4
GLA_GATE_NORMALIZER = 16.0
ATTN_HEADS = 16
LEFT_CHUNKS = 8
REL_CLIP = 2 * CHUNK
LRU_HEADS = 4
CONV_WIDTH = 4
LRU_C = 8.0
LOG2E = math.log2(math.e)

LANES = 128
VMEM_LIMIT_BYTES = 56 * 1024 * 1024

FFN_ROWS = 512
FFN_COLS = 256
GLA_CHUNK = 64
GLA_ROWS = 512
ATTN_ROWS = 512
LRU_ROWS = 512


def _const_spec(shape):
    nd = len(shape)
    return pl.BlockSpec(shape, lambda *_: (0,) * nd, pipeline_mode=pl.Buffered(1))


def _rms(x, gain):
    ms = jnp.mean(x * x, axis=-1, keepdims=True)
    return x * lax.rsqrt(ms + EPS) * gain


def _silu(x):
    return x * jax.nn.sigmoid(x)


def _dot(a, b):
    return jnp.dot(a, b, preferred_element_type=F32)


def _dot_nt(a, b):
    return lax.dot_general(a, b, (((1,), (1,)), ((), ())), preferred_element_type=F32)


def _dot_tn(a, b):
    return lax.dot_general(a, b, (((0,), (0,)), ((), ())), preferred_element_type=F32)


def _ffn_kernel(x_ref, gain_ref, wgu_ref, wd_ref, fgain_ref, o_ref, *, d_ff, cols, final_norm):
    x = x_ref[...]
    h = _rms(x, gain_ref[...]).astype(BF16)
    acc = None
    for f in range(d_ff // cols):
        g = _dot(h, wgu_ref[:, f * cols:(f + 1) * cols])
        u = _dot(h, wgu_ref[:, d_ff + f * cols:d_ff + (f + 1) * cols])
        a = (_silu(g) * u).astype(BF16)
        part = _dot(a, wd_ref[f * cols:(f + 1) * cols, :])
        acc = part if acc is None else acc + part
    y = x + 0.5 * acc
    if final_norm:
        y = _rms(y, fgain_ref[...])
    o_ref[...] = y


def _ffn(x2, gain, w_gate_up, w_down, final_gain, final_norm):
    t, d = x2.shape
    d_ff = w_down.shape[0]
    rows = min(FFN_ROWS, t)
    cols = FFN_COLS if d_ff % FFN_COLS == 0 else d_ff
    assert t % rows == 0
    kern = functools.partial(_ffn_kernel, d_ff=d_ff, cols=cols, final_norm=final_norm)
    return pl.pallas_call(
        kern,
        grid=(t // rows,),
        in_specs=[
            pl.BlockSpec((rows, d), lambda i: (i, 0)),
            _const_spec((1, d)),
            _const_spec((d, 2 * d_ff)),
            _const_spec((d_ff, d)),
            _const_spec((1, d)),
        ],
        out_specs=pl.BlockSpec((rows, d), lambda i: (i, 0)),
        out_shape=jax.ShapeDtypeStruct((t, d), F32),
        compiler_params=pltpu.CompilerParams(
            dimension_semantics=("parallel",), vmem_limit_bytes=VMEM_LIMIT_BYTES),
        name="ffn",
    )(x2, gain.reshape(1, d), w_gate_up.astype(BF16), w_down.astype(BF16), final_gain.reshape(1, d))


def _gla_level_count(chunk):
    return int(math.log2(chunk))


def _gla_decay_matrices(chunk):
    levels = _gla_level_count(chunk)
    t = np.arange(chunk)[:, None]
    j = np.arange(chunk)[None, :]
    mats = []
    for lvl in range(levels):
        half = 1 << lvl
        ref = (t // (2 * half)) * (2 * half) + half - 1
        upper = t > ref
        mats.append(np.where(upper, (j > ref) & (j <= t), (j > t) & (j <= ref)))
    mats.append(j <= t)
    mats.append(j > t)
    d = np.stack(mats).astype(np.float32)
    return np.concatenate([d, d, d], axis=-1)


def _split3(g):
    g1 = g.astype(BF16)
    r1 = g - g1.astype(F32)
    g2 = r1.astype(BF16)
    g3 = (r1 - g2.astype(F32)).astype(BF16)
    return jnp.concatenate([g1, g2, g3], axis=0)


def _gla_chunk_loop(q_ref, k_ref, v_ref, g_ref, og_ref, dmat_ref, ngain_ref, st_ref, y_ref,
                    *, rows, heads, dk, dv):
    c = GLA_CHUNK
    levels = _gla_level_count(c)
    ti = lax.broadcasted_iota(jnp.int32, (c, c), 0)
    si = lax.broadcasted_iota(jnp.int32, (c, c), 1)
    diag_mask = ti == si
    level_masks = []
    for lvl in range(levels):
        same = (ti >> (lvl + 1)) == (si >> (lvl + 1))
        level_masks.append(same & (((ti >> lvl) & 1) == 1) & (((si >> lvl) & 1) == 0))
    ngain = ngain_ref[...]

    def body(ci, carry):
        r0 = pl.multiple_of(ci * c, c)
        rs = pl.ds(r0, c)
        g3 = _split3(g_ref[rs, :])
        ex = jnp.exp(_dot(dmat_ref[...], g3))
        e_cum = ex[levels * c:(levels + 1) * c]
        e_rev = ex[(levels + 1) * c:(levels + 2) * c]
        q = q_ref[rs, :]
        k = k_ref[rs, :]
        a_list = []
        for h in range(heads):
            ks = slice(h * dk, (h + 1) * dk)
            qh, kh = q[:, ks], k[:, ks]
            a = jnp.where(diag_mask, _dot_nt(qh.astype(BF16), kh.astype(BF16)), 0.0)
            for lvl in range(levels):
                e = ex[lvl * c:(lvl + 1) * c, ks]
                s = _dot_nt((qh * e).astype(BF16), (kh * e).astype(BF16))
                a = a + jnp.where(level_masks[lvl], s, 0.0)
            a_list.append(a.astype(BF16))
        for h in range(heads):
            ks = slice(h * dk, (h + 1) * dk)
            vs = slice(h * dv, (h + 1) * dv)
            qh, kh = q[:, ks], k[:, ks]
            vh = v_ref[rs, vs]
            st = st_ref[h]
            o = _dot(a_list[h], vh) + _dot_nt((qh * e_cum[:, ks]).astype(BF16), st.astype(BF16))
            e_last = e_cum[c - 1:c, ks]
            st_ref[h] = st * e_last + _dot_tn(vh, (kh * e_rev[:, ks]).astype(BF16))
            o = _rms(o, ngain) * _silu(og_ref[rs, vs])
            y_ref[rs, vs] = o.astype(BF16)
        return carry

    lax.fori_loop(0, rows // c, body, 0)


def _hgrn_kernel(x_ref, gain_ref, win_ref, lb_ref, dmat_ref, ngain_ref, wout_ref, o_ref,
                 q_s, k_s, g_s, v_s, og_s, y_s, st_s, *, rows, heads, dk):
    d = heads * dk

    @pl.when(pl.program_id(1) == 0)
    def _():
        st_s[...] = jnp.zeros_like(st_s)

    x = x_ref[...]
    h = _rms(x, gain_ref[...]).astype(BF16)
    q_s[...] = _silu(_dot(h, win_ref[:, 0:d])) * (dk ** -0.5)
    lb = lb_ref[...]
    fgate = lb + (1.0 - lb) * jax.nn.sigmoid(_dot(h, win_ref[:, d:2 * d]))
    k_s[...] = 1.0 - fgate
    g_s[...] = jnp.log(fgate)
    v_s[...] = _dot(h, win_ref[:, 2 * d:3 * d]).astype(BF16)
    og_s[...] = _dot(h, win_ref[:, 3 * d:4 * d])
    _gla_chunk_loop(q_s, k_s, v_s, g_s, og_s, dmat_ref, ngain_ref, st_s, y_s,
                    rows=rows, heads=heads, dk=dk, dv=dk)
    o_ref[...] = x + _dot(y_s[...], wout_ref[...])


def _seq_call(kern, x, consts, scratch, rows, name):
    b, l, d = x.shape
    assert l % rows == 0
    xspec = pl.BlockSpec((None, rows, d), lambda bi, ti: (bi, ti, 0))
    return pl.pallas_call(
        kern,
        grid=(b, l // rows),
        in_specs=[xspec] + [_const_spec(c.shape) for c in consts],
        out_specs=xspec,
        out_shape=jax.ShapeDtypeStruct(x.shape, F32),
        scratch_shapes=scratch,
        compiler_params=pltpu.CompilerParams(
            dimension_semantics=("arbitrary", "arbitrary"), vmem_limit_bytes=VMEM_LIMIT_BYTES),
        name=name,
    )(x, *consts)


def _hgrn_layer(x, gain, w_in, lower_bound, norm_gain, w_out):
    b, l, d = x.shape
    heads, dk = HGRN_HEADS, d // HGRN_HEADS
    rows = min(GLA_ROWS, l)
    dmat = jnp.asarray(_gla_decay_matrices(GLA_CHUNK), BF16).reshape(-1, 3 * GLA_CHUNK)
    consts = [gain.reshape(1, d), w_in.astype(BF16), lower_bound.reshape(1, d), dmat,
              norm_gain.reshape(1, dk), w_out.astype(BF16)]
    scratch = [pltpu.VMEM((rows, d), F32), pltpu.VMEM((rows, d), F32), pltpu.VMEM((rows, d), F32),
               pltpu.VMEM((rows, d), BF16), pltpu.VMEM((rows, d), F32), pltpu.VMEM((rows, d), BF16),
               pltpu.VMEM((heads, dk, dk), F32)]
    kern = functools.partial(_hgrn_kernel, rows=rows, heads=heads, dk=dk)
    return _seq_call(kern, x, consts, scratch, rows, "hgrn_mixer")


def _log_sigmoid(x):
    return jnp.minimum(x, 0.0) - jnp.log(1.0 + jnp.exp(-jnp.abs(x)))


def _gla_kernel(x_ref, gain_ref, win_ref, wgd_ref, wgu_ref, gbias_ref, dmat_ref, ngain_ref, wout_ref,
                o_ref, q_s, k_s, g_s, v_s, og_s, y_s, st_s, *, rows, heads, dk, dv):
    kd, vd = heads * dk, heads * dv

    @pl.when(pl.program_id(1) == 0)
    def _():
        st_s[...] = jnp.zeros_like(st_s)

    x = x_ref[...]
    h = _rms(x, gain_ref[...]).astype(BF16)
    q_s[...] = _dot(h, win_ref[:, 0:kd]) * (dk ** -0.5)
    k_s[...] = _dot(h, win_ref[:, kd:2 * kd])
    v_s[...] = _dot(h, win_ref[:, 2 * kd:2 * kd + vd]).astype(BF16)
    og_s[...] = _dot(h, win_ref[:, 2 * kd + vd:2 * kd + 2 * vd])
    low = _dot(h, wgd_ref[...]).astype(BF16)
    logits = _dot(low, wgu_ref[...]) + gbias_ref[...]
    g_s[...] = _log_sigmoid(logits) * (1.0 / GLA_GATE_NORMALIZER)
    _gla_chunk_loop(q_s, k_s, v_s, g_s, og_s, dmat_ref, ngain_ref, st_s, y_s,
                    rows=rows, heads=heads, dk=dk, dv=dv)
    o_ref[...] = x + _dot(y_s[...], wout_ref[...])


def _gla_layer(x, gain, w_in, w_gate_down, w_gate_up, gate_bias, norm_gain, w_out):
    b, l, d = x.shape
    kd = gate_bias.shape[0]
    vd = w_out.shape[0]
    heads = GLA_HEADS
    dk, dv = kd // heads, vd // heads
    rank = w_gate_down.shape[1]
    rank_pad = -(-rank // LANES) * LANES
    wgd = jnp.pad(w_gate_down, ((0, 0), (0, rank_pad - rank))).astype(BF16)
    wgu = jnp.pad(w_gate_up, ((0, rank_pad - rank), (0, 0))).astype(BF16)
    rows = min(GLA_ROWS, l)
    dmat = jnp.asarray(_gla_decay_matrices(GLA_CHUNK), BF16).reshape(-1, 3 * GLA_CHUNK)
    consts = [gain.reshape(1, d), w_in.astype(BF16), wgd, wgu, gate_bias.reshape(1, kd), dmat,
              norm_gain.reshape(1, dv), w_out.astype(BF16)]
    scratch = [pltpu.VMEM((rows, kd), F32), pltpu.VMEM((rows, kd), F32), pltpu.VMEM((rows, kd), F32),
               pltpu.VMEM((rows, vd), BF16), pltpu.VMEM((rows, vd), F32), pltpu.VMEM((rows, vd), BF16),
               pltpu.VMEM((heads, dv, dk), F32)]
    kern = functools.partial(_gla_kernel, rows=rows, heads=heads, dk=dk, dv=dv)
    return _seq_call(kern, x, consts, scratch, rows, "gla_mixer")


def _attn_kernel(x_ref, gain_ref, wqkv_ref, bias_ref, wout_ref, o_ref, q_s, k_s, v_s, y_s,
                 *, rows, hist, d, dh):
    tb = pl.program_id(1)
    pair = 2 * CHUNK
    band = hist + pair
    npairs = d // LANES

    @pl.when(tb == 0)
    def _():
        k_s[0:hist, :] = jnp.zeros((hist, d), BF16)
        v_s[0:hist, :] = jnp.zeros((hist, d), BF16)

    x = x_ref[...]
    h = _rms(x, gain_ref[...]).astype(BF16)
    q_s[...] = (_dot(h, wqkv_ref[:, 0:d]) * (dh ** -0.5 * LOG2E)).astype(BF16)
    k_s[hist:hist + rows, :] = _dot(h, wqkv_ref[:, d:2 * d]).astype(BF16)
    v_s[hist:hist + rows, :] = _dot(h, wqkv_ref[:, 2 * d:3 * d]).astype(BF16)

    lane = lax.broadcasted_iota(jnp.int32, (pair, LANES), 1)
    first_head = lane < dh
    col = lax.broadcasted_iota(jnp.int32, (8, band), 1)

    def body(p, carry):
        r0 = pl.multiple_of(p * pair, pair)
        key_mask = jnp.where(col + (tb * rows + r0 - hist) >= 0, 0.0, NEG_INF)
        key_mask = jnp.concatenate([key_mask] * (2 * pair // 8), axis=0)

        def scores(hp):
            ls = slice(hp * LANES, (hp + 1) * LANES)
            q2 = q_s[pl.ds(r0, pair), ls]
            zero = jnp.zeros_like(q2)
            qq = jnp.concatenate([jnp.where(first_head, q2, zero), jnp.where(first_head, zero, q2)], axis=0)
            return _dot_nt(qq, k_s[pl.ds(r0, band), ls]) + bias_ref[hp] + key_mask

        s = scores(0)
        for hp in range(npairs):
            s_next = scores(hp + 1) if hp + 1 < npairs else None
            ls = slice(hp * LANES, (hp + 1) * LANES)
            m = jnp.max(s, axis=-1, keepdims=True)
            e = jnp.exp2(s - m)
            den = jnp.sum(e, axis=-1, keepdims=True)
            o2 = _dot(e.astype(BF16), v_s[pl.ds(r0, band), ls]) / den
            y_s[pl.ds(r0, pair), ls] = jnp.where(first_head, o2[0:pair], o2[pair:2 * pair]).astype(BF16)
            s = s_next
        return carry

    lax.fori_loop(0, rows // pair, body, 0)
    o_ref[...] = x + _dot(y_s[...], wout_ref[...])
    k_s[0:hist, :] = k_s[rows:rows + hist, :]
    v_s[0:hist, :] = v_s[rows:rows + hist, :]


def _attn_bias(rel_table):
    heads = rel_table.shape[0]
    hist = LEFT_CHUNKS * CHUNK
    band1 = hist + CHUNK
    ext_idx = jnp.clip(band1 - 1 - jnp.arange(band1 + CHUNK - 1), -(CHUNK - 1), REL_CLIP) + (CHUNK - 1)
    ext = rel_table.astype(F32)[:, ext_idx] * LOG2E
    bias = jnp.stack([ext[:, CHUNK - 1 - t:CHUNK - 1 - t + band1] for t in range(CHUNK)], axis=1)
    neg = jnp.full((heads, CHUNK, CHUNK), NEG_INF, F32)
    both = jnp.concatenate([jnp.concatenate([bias, neg], axis=-1),
                            jnp.concatenate([neg, bias], axis=-1)], axis=1)
    return both.reshape(heads // 2, 4 * CHUNK, band1 + CHUNK)


def _attn_layer(x, gain, w_qkv, rel_table, w_out):
    b, l, d = x.shape
    dh = d // ATTN_HEADS
    hist = LEFT_CHUNKS * CHUNK
    rows = min(ATTN_ROWS, l)
    assert 2 * dh == LANES and rows % (2 * CHUNK) == 0 and rows >= hist
    consts = [gain.reshape(1, d), w_qkv.astype(BF16), _attn_bias(rel_table), w_out.astype(BF16)]
    scratch = [pltpu.VMEM((rows, d), BF16), pltpu.VMEM((hist + rows, d), BF16),
               pltpu.VMEM((hist + rows, d), BF16), pltpu.VMEM((rows, d), BF16)]
    kern = functools.partial(_attn_kernel, rows=rows, hist=hist, d=d, dh=dh)
    return _seq_call(kern, x, consts, scratch, rows, "attn_mixer")


def _gelu_tanh(x):
    return x * jax.nn.sigmoid((2.0 * math.sqrt(2.0 / math.pi)) * (x + 0.044715 * (x * x * x)))


def _lru_kernel(x_ref, gain_ref, win_ref, convw_ref, convb_ref, wa_ref, ba_ref, wx_ref, bx_ref,
                lam_ref, wout_ref, o_ref, tail_s, h_s, a_s, b_s, c_s, *, rows, w, blocks):
    tb = pl.program_id(1)
    tail = tail_s.shape[0]

    @pl.when(tb == 0)
    def _():
        tail_s[...] = jnp.zeros_like(tail_s)
        h_s[...] = jnp.zeros_like(h_s)

    x = x_ref[...]
    h = _rms(x, gain_ref[...]).astype(BF16)
    y = _gelu_tanh(_dot(h, win_ref[:, 0:w]))
    xb = _dot(h, win_ref[:, w:2 * w])

    ext = jnp.concatenate([tail_s[...], xb], axis=0)
    tail_s[...] = xb[rows - tail:rows, :]
    convw = convw_ref[...]
    xc = xb * convw[CONV_WIDTH - 1:CONV_WIDTH, :] + convb_ref[...]
    for i in range(CONV_WIDTH - 1):
        shift = CONV_WIDTH - 1 - i
        xc = xc + pltpu.roll(ext, shift, axis=0)[tail:tail + rows, :] * convw[i:i + 1, :]

    bw = w // blocks
    xcb = xc.astype(BF16)
    ra = jnp.concatenate([_dot(xcb[:, j * bw:(j + 1) * bw], wa_ref[j]) for j in range(blocks)], axis=-1)
    ri = jnp.concatenate([_dot(xcb[:, j * bw:(j + 1) * bw], wx_ref[j]) for j in range(blocks)], axis=-1)
    r = jax.nn.sigmoid(ra + ba_ref[...])
    gi = jax.nn.sigmoid(ri + bx_ref[...])
    lam = lam_ref[...]
    softplus_neg = jnp.maximum(-lam, 0.0) + jnp.log(1.0 + jnp.exp(-jnp.abs(lam)))
    log_a = -LRU_C * r * softplus_neg
    a = jnp.exp(log_a)
    th = jnp.tanh(log_a)
    z = -2.0 * th / (1.0 - th)
    mult = jnp.where(z > 0.0, z * lax.rsqrt(z), 0.0)
    row = lax.broadcasted_iota(jnp.int32, (rows, w), 0)
    mult = jnp.where((row + tb * rows) == 0, 1.0, mult)
    bb = mult * gi * xc

    groups = rows // 8
    a = a.reshape(groups, 8, w)
    bb = bb.reshape(groups, 8, w)
    sub = lax.broadcasted_iota(jnp.int32, (groups, 8, w), 1)
    shift = 1
    while shift < 8:
        keep = sub >= shift
        a_prev = jnp.where(keep, pltpu.roll(a, shift, axis=1), 1.0)
        b_prev = jnp.where(keep, pltpu.roll(bb, shift, axis=1), 0.0)
        bb = a * b_prev + bb
        a = a * a_prev
        shift *= 2
    a_s[...] = a.reshape(rows, w)
    b_s[...] = bb.reshape(rows, w)
    carry = h_s[...]
    for g in range(groups):
        c_s[g:g + 1, :] = carry
        last = 8 * g + 7
        carry = b_s[last:last + 1, :] + a_s[last:last + 1, :] * carry
    h_s[...] = carry
    hs = jnp.concatenate(
        [b_s[8 * g:8 * g + 8, :] + a_s[8 * g:8 * g + 8, :] * c_s[g:g + 1, :] for g in range(groups)], axis=0)
    o_ref[...] = x + _dot((y * hs).astype(BF16), wout_ref[...])


def _lru_layer(x, gain, w_in, conv_w, conv_b, w_a, b_a, w_x, b_x, lam, w_out):
    b, l, d = x.shape
    w = w_out.shape[0]
    blocks = w_a.shape[0]
    rows = min(LRU_ROWS, l)
    row = lambda v: v.reshape(1, -1)
    consts = [row(gain), w_in.astype(BF16), conv_w, row(conv_b), w_a.astype(BF16), row(b_a),
              w_x.astype(BF16), row(b_x), row(lam), w_out.astype(BF16)]
    scratch = [pltpu.VMEM((8, w), F32), pltpu.VMEM((1, w), F32), pltpu.VMEM((rows, w), F32),
               pltpu.VMEM((rows, w), F32), pltpu.VMEM((rows // 8, w), F32)]
    kern = functools.partial(_lru_kernel, rows=rows, w=w, blocks=blocks)
    return _seq_call(kern, x, consts, scratch, rows, "lru_mixer")


def kernel(x, ffn_w_gate_up, ffn_w_down, norm_gains, final_norm_gain, hgrn_w_in, hgrn_lower_bound_logits, hgrn_norm_gain, hgrn_w_out, gla_w_in, gla_w_gate_down, gla_w_gate_up, gla_gate_bias, gla_norm_gain, gla_w_out, attn_w_qkv, attn_rel_bias, attn_w_out, lru_w_in, lru_conv_w, lru_conv_b, lru_w_a, lru_b_a, lru_w_x, lru_b_x, lru_lambda, lru_w_out):
    b, l, d = x.shape
    depth = ffn_w_gate_up.shape[0]
    lower_bounds = jnp.cumsum(jax.nn.softmax(hgrn_lower_bound_logits.astype(F32), axis=0), axis=0)

    def ffn(xx, i, s, final):
        y = _ffn(xx.reshape(b * l, d), norm_gains[i, 2 * s], ffn_w_gate_up[i, s], ffn_w_down[i, s],
                 final_norm_gain, final)
        return y.reshape(b, l, d)

    for i in range(depth):
        m, j = i % N_MIXERS, i // N_MIXERS
        x = ffn(x, i, 0, False)
        g = norm_gains[i, 1]
        if m == 0:
            x = _hgrn_layer(x, g, hgrn_w_in[j], lower_bounds[i], hgrn_norm_gain[j], hgrn_w_out[j])
        elif m == 1:
            x = _gla_layer(x, g, gla_w_in[j], gla_w_gate_down[j], gla_w_gate_up[j], gla_gate_bias[j],
                           gla_norm_gain[j], gla_w_out[j])
        elif m == 2:
            x = _attn_layer(x, g, attn_w_qkv[j], attn_rel_bias[j], attn_w_out[j])
        else:
            x = _lru_layer(x, g, lru_w_in[j], lru_conv_w[j], lru_conv_b[j], lru_w_a[j], lru_b_a[j],
                           lru_w_x[j], lru_b_x[j], lru_lambda[j], lru_w_out[j])
        x = ffn(x, i, 1, i == depth - 1)
    return x
```

```python
import functools
import math

import numpy as np
import jax
import jax.numpy as jnp
from jax import lax
from jax.experimental import pallas as pl
from jax.experimental.pallas import tpu as pltpu

F32 = jnp.float32
BF16 = jnp.bfloat16

EPS = 1e-6
CHUNK = 64
N_MIXERS = 4
NEG_INF = -1e30
HGRN_HEADS = 8
GLA_HEADS = 4
GLA_GATE_NORMALIZER = 16.0
ATTN_HEADS = 16
LEFT_CHUNKS = 8
REL_CLIP = 2 * CHUNK
LRU_HEADS = 4
CONV_WIDTH = 4
LRU_C = 8.0
LOG2E = math.log2(math.e)

LANES = 128
SUBLANES = 8
VMEM_LIMIT_BYTES = 56 * 1024 * 1024

FFN_ROWS = 512
FFN_COLS = 256
GLA_CHUNK = 64
GLA_ROWS = 512
ATTN_ROWS = 512
LRU_ROWS = 512


def _const_spec(shape):
    nd = len(shape)
    return pl.BlockSpec(shape, lambda *_: (0,) * nd, pipeline_mode=pl.Buffered(1))


def _rms(x, gain):
    ms = jnp.mean(x * x, axis=-1, keepdims=True)
    return x * lax.rsqrt(ms + EPS) * gain


def _silu(x):
    return x * jax.nn.sigmoid(x)


def _dot(a, b):
    return jnp.dot(a, b, preferred_element_type=F32)


def _dot_nt(a, b):
    return lax.dot_general(a, b, (((1,), (1,)), ((), ())), preferred_element_type=F32)


def _dot_tn(a, b):
    return lax.dot_general(a, b, (((0,), (0,)), ((), ())), preferred_element_type=F32)


def _ffn_kernel(x_ref, gain_ref, wgu_ref, wd_ref, fgain_ref, o_ref, *, d_ff, cols, final_norm):
    x = x_ref[...]
    h = _rms(x, gain_ref[...]).astype(BF16)
    acc = None
    for f in range(d_ff // cols):
        g = _dot(h, wgu_ref[:, f * cols:(f + 1) * cols])
        u = _dot(h, wgu_ref[:, d_ff + f * cols:d_ff + (f + 1) * cols])
        a = (_silu(g) * u).astype(BF16)
        part = _dot(a, wd_ref[f * cols:(f + 1) * cols, :])
        acc = part if acc is None else acc + part
    y = x + 0.5 * acc
    if final_norm:
        y = _rms(y, fgain_ref[...])
    o_ref[...] = y


def _ffn(x2, gain, w_gate_up, w_down, final_gain, final_norm):
    t, d = x2.shape
    d_ff = w_down.shape[0]
    rows = min(FFN_ROWS, t)
    cols = FFN_COLS if d_ff % FFN_COLS == 0 else d_ff
    assert t % rows == 0
    kern = functools.partial(_ffn_kernel, d_ff=d_ff, cols=cols, final_norm=final_norm)
    return pl.pallas_call(
        kern,
        grid=(t // rows,),
        in_specs=[
            pl.BlockSpec((rows, d), lambda i: (i, 0)),
            _const_spec((1, d)),
            _const_spec((d, 2 * d_ff)),
            _const_spec((d_ff, d)),
            _const_spec((1, d)),
        ],
        out_specs=pl.BlockSpec((rows, d), lambda i: (i, 0)),
        out_shape=jax.ShapeDtypeStruct((t, d), F32),
        compiler_params=pltpu.CompilerParams(
            dimension_semantics=("parallel",), vmem_limit_bytes=VMEM_LIMIT_BYTES),
        name="ffn",
    )(x2, gain.reshape(1, d), w_gate_up.astype(BF16), w_down.astype(BF16), final_gain.reshape(1, d))


def _gla_level_count(chunk):
    return int(math.log2(chunk))


def _gla_decay_matrices(chunk):
    t = np.arange(chunk)[:, None]
    j = np.arange(chunk)[None, :]
    mats = [j <= t]
    for lvl in range(_gla_level_count(chunk)):
        half = 1 << lvl
        if half >= SUBLANES:
            break
        ref = (t // (2 * half)) * (2 * half) + half - 1
        mats.append(np.where(t > ref, (j > ref) & (j <= t), (j > t) & (j <= ref)))
    d = np.concatenate(mats, axis=0).astype(np.float32)
    return np.concatenate([d, d, d], axis=-1)


def _split3(g):
    g1 = g.astype(BF16)
    r1 = g - g1.astype(F32)
    g2 = r1.astype(BF16)
    g3 = (r1 - g2.astype(F32)).astype(BF16)
    return jnp.concatenate([g1, g2, g3], axis=0)


def _gla_chunk_loop(q_ref, k_ref, v_ref, g_ref, og_ref, dmat_ref, ngain_ref, st_ref, y_ref,
                    *, rows, heads, dk, dv):
    c = GLA_CHUNK
    levels = _gla_level_count(c)
    small = [lvl for lvl in range(levels) if (1 << lvl) < SUBLANES]
    ti = lax.broadcasted_iota(jnp.int32, (c, c), 0)
    si = lax.broadcasted_iota(jnp.int32, (c, c), 1)
    diag_mask = ti == si
    level_masks = []
    for lvl in range(levels):
        same = (ti >> (lvl + 1)) == (si >> (lvl + 1))
        level_masks.append(same & (((ti >> lvl) & 1) == 1) & (((si >> lvl) & 1) == 0))
    ngain = ngain_ref[...]

    def body(ci, carry):
        rs = pl.ds(ci * c, c)
        g3 = _split3(g_ref[rs, :])
        xs = _dot(dmat_ref[...], g3)
        cum = xs[0:c]
        ex = {lvl: jnp.exp(xs[(i + 1) * c:(i + 2) * c]) for i, lvl in enumerate(small)}
        for lvl in range(levels):
            if lvl in ex:
                continue
            half = 1 << lvl
            parts = []
            for b0 in range(0, c, 2 * half):
                ref = cum[b0 + half - 1:b0 + half, :]
                parts.append(ref - cum[b0:b0 + half])
                parts.append(cum[b0 + half:b0 + 2 * half] - ref)
            ex[lvl] = jnp.exp(jnp.concatenate(parts, axis=0))
        e_cum = jnp.exp(cum)
        e_rev = jnp.exp(cum[c - 1:c, :] - cum)
        q = q_ref[rs, :]
        k = k_ref[rs, :]
        a_list = []
        for h in range(heads):
            ks = slice(h * dk, (h + 1) * dk)
            qh, kh = q[:, ks], k[:, ks]
            a = jnp.where(diag_mask, _dot_nt(qh.astype(BF16), kh.astype(BF16)), 0.0)
            for lvl in range(levels):
                e = ex[lvl][:, ks]
                s = _dot_nt((qh * e).astype(BF16), (kh * e).astype(BF16))
                a = a + jnp.where(level_masks[lvl], s, 0.0)
            a_list.append(a.astype(BF16))
        for h in range(heads):
            ks = slice(h * dk, (h + 1) * dk)
            vs = slice(h * dv, (h + 1) * dv)
            qh, kh = q[:, ks], k[:, ks]
            vh = v_ref[rs, vs]
            st = st_ref[h]
            o = _dot(a_list[h], vh) + _dot_nt((qh * e_cum[:, ks]).astype(BF16), st.astype(BF16))
            e_last = e_cum[c - 1:c, ks]
            st_ref[h] = st * e_last + _dot_tn(vh, (kh * e_rev[:, ks]).astype(BF16))
            o = _rms(o, ngain) * _silu(og_ref[rs, vs])
            y_ref[rs, vs] = o.astype(BF16)
        return carry

    for ci in range(rows // c):
        body(ci, 0)


def _hgrn_kernel(x_ref, gain_ref, win_ref, lb_ref, dmat_ref, ngain_ref, wout_ref, o_ref,
                 q_s, k_s, g_s, v_s, og_s, y_s, st_s, *, rows, heads, dk):
    d = heads * dk

    @pl.when(pl.program_id(1) == 0)
    def _():
        st_s[...] = jnp.zeros_like(st_s)

    x = x_ref[...]
    h = _rms(x, gain_ref[...]).astype(BF16)
    q_s[...] = _silu(_dot(h, win_ref[:, 0:d])) * (dk ** -0.5)
    lb = lb_ref[...]
    fgate = lb + (1.0 - lb) * jax.nn.sigmoid(_dot(h, win_ref[:, d:2 * d]))
    k_s[...] = 1.0 - fgate
    g_s[...] = jnp.log(fgate)
    v_s[...] = _dot(h, win_ref[:, 2 * d:3 * d]).astype(BF16)
    og_s[...] = _dot(h, win_ref[:, 3 * d:4 * d])
    _gla_chunk_loop(q_s, k_s, v_s, g_s, og_s, dmat_ref, ngain_ref, st_s, y_s,
                    rows=rows, heads=heads, dk=dk, dv=dk)
    o_ref[...] = x + _dot(y_s[...], wout_ref[...])


def _seq_call(kern, x, consts, scratch, rows, name):
    b, l, d = x.shape
    assert l % rows == 0
    xspec = pl.BlockSpec((None, rows, d), lambda bi, ti: (bi, ti, 0))
    return pl.pallas_call(
        kern,
        grid=(b, l // rows),
        in_specs=[xspec] + [_const_spec(c.shape) for c in consts],
        out_specs=xspec,
        out_shape=jax.ShapeDtypeStruct(x.shape, F32),
        scratch_shapes=scratch,
        compiler_params=pltpu.CompilerParams(
            dimension_semantics=("arbitrary", "arbitrary"), vmem_limit_bytes=VMEM_LIMIT_BYTES),
        name=name,
    )(x, *consts)


def _hgrn_layer(x, gain, w_in, lower_bound, norm_gain, w_out):
    b, l, d = x.shape
    heads, dk = HGRN_HEADS, d // HGRN_HEADS
    rows = min(GLA_ROWS, l)
    dmat = jnp.asarray(_gla_decay_matrices(GLA_CHUNK), BF16)
    consts = [gain.reshape(1, d), w_in.astype(BF16), lower_bound.reshape(1, d), dmat,
              norm_gain.reshape(1, dk), w_out.astype(BF16)]
    scratch = [pltpu.VMEM((rows, d), F32), pltpu.VMEM((rows, d), F32), pltpu.VMEM((rows, d), F32),
               pltpu.VMEM((rows, d), BF16), pltpu.VMEM((rows, d), F32), pltpu.VMEM((rows, d), BF16),
               pltpu.VMEM((heads, dk, dk), F32)]
    kern = functools.partial(_hgrn_kernel, rows=rows, heads=heads, dk=dk)
    return _seq_call(kern, x, consts, scratch, rows, "hgrn_mixer")


def _log_sigmoid(x):
    return jnp.minimum(x, 0.0) - jnp.log(1.0 + jnp.exp(-jnp.abs(x)))


def _gla_kernel(x_ref, gain_ref, win_ref, wgd_ref, wgu_ref, gbias_ref, dmat_ref, ngain_ref, wout_ref,
                o_ref, q_s, k_s, g_s, v_s, og_s, y_s, st_s, *, rows, heads, dk, dv):
    kd, vd = heads * dk, heads * dv

    @pl.when(pl.program_id(1) == 0)
    def _():
        st_s[...] = jnp.zeros_like(st_s)

    x = x_ref[...]
    h = _rms(x, gain_ref[...]).astype(BF16)
    q_s[...] = _dot(h, win_ref[:, 0:kd]) * (dk ** -0.5)
    k_s[...] = _dot(h, win_ref[:, kd:2 * kd])
    v_s[...] = _dot(h, win_ref[:, 2 * kd:2 * kd + vd]).astype(BF16)
    og_s[...] = _dot(h, win_ref[:, 2 * kd + vd:2 * kd + 2 * vd])
    low = _dot(h, wgd_ref[...]).astype(BF16)
    logits = _dot(low, wgu_ref[...]) + gbias_ref[...]
    g_s[...] = _log_sigmoid(logits) * (1.0 / GLA_GATE_NORMALIZER)
    _gla_chunk_loop(q_s, k_s, v_s, g_s, og_s, dmat_ref, ngain_ref, st_s, y_s,
                    rows=rows, heads=heads, dk=dk, dv=dv)
    o_ref[...] = x + _dot(y_s[...], wout_ref[...])


def _gla_layer(x, gain, w_in, w_gate_down, w_gate_up, gate_bias, norm_gain, w_out):
    b, l, d = x.shape
    kd = gate_bias.shape[0]
    vd = w_out.shape[0]
    heads = GLA_HEADS
    dk, dv = kd // heads, vd // heads
    rank = w_gate_down.shape[1]
    rank_pad = -(-rank // LANES) * LANES
    wgd = jnp.pad(w_gate_down, ((0, 0), (0, rank_pad - rank))).astype(BF16)
    wgu = jnp.pad(w_gate_up, ((0, rank_pad - rank), (0, 0))).astype(BF16)
    rows = min(GLA_ROWS, l)
    dmat = jnp.asarray(_gla_decay_matrices(GLA_CHUNK), BF16)
    consts = [gain.reshape(1, d), w_in.astype(BF16), wgd, wgu, gate_bias.reshape(1, kd), dmat,
              norm_gain.reshape(1, dv), w_out.astype(BF16)]
    scratch = [pltpu.VMEM((rows, kd), F32), pltpu.VMEM((rows, kd), F32), pltpu.VMEM((rows, kd), F32),
               pltpu.VMEM((rows, vd), BF16), pltpu.VMEM((rows, vd), F32), pltpu.VMEM((rows, vd), BF16),
               pltpu.VMEM((heads, dv, dk), F32)]
    kern = functools.partial(_gla_kernel, rows=rows, heads=heads, dk=dk, dv=dv)
    return _seq_call(kern, x, consts, scratch, rows, "gla_mixer")


def _attn_kernel(x_ref, gain_ref, wqkv_ref, bias_ref, wout_ref, o_ref, q_s, k_s, v_s, y_s,
                 *, rows, hist, d, dh):
    tb = pl.program_id(1)
    pair = 2 * CHUNK
    band = hist + pair
    npairs = d // LANES

    @pl.when(tb == 0)
    def _():
        k_s[0:hist, :] = jnp.zeros((hist, d), BF16)
        v_s[0:hist, :] = jnp.zeros((hist, d), BF16)

    x = x_ref[...]
    h = _rms(x, gain_ref[...]).astype(BF16)
    q_s[...] = (_dot(h, wqkv_ref[:, 0:d]) * (dh ** -0.5 * LOG2E)).astype(BF16)
    k_s[hist:hist + rows, :] = _dot(h, wqkv_ref[:, d:2 * d]).astype(BF16)
    v_s[hist:hist + rows, :] = _dot(h, wqkv_ref[:, 2 * d:3 * d]).astype(BF16)

    lane = lax.broadcasted_iota(jnp.int32, (pair, LANES), 1)
    first_head = lane < dh
    col = lax.broadcasted_iota(jnp.int32, (8, band), 1)

    def body(p, carry):
        r0 = p * pair
        key_mask = jnp.where(col + (tb * rows + r0 - hist) >= 0, 0.0, NEG_INF)
        key_mask = jnp.concatenate([key_mask] * (2 * pair // 8), axis=0)

        def scores(hp):
            ls = slice(hp * LANES, (hp + 1) * LANES)
            q2 = q_s[pl.ds(r0, pair), ls]
            zero = jnp.zeros_like(q2)
            qq = jnp.concatenate([jnp.where(first_head, q2, zero), jnp.where(first_head, zero, q2)], axis=0)
            return _dot_nt(qq, k_s[pl.ds(r0, band), ls]) + bias_ref[hp] + key_mask

        s = scores(0)
        for hp in range(npairs):
            s_next = scores(hp + 1) if hp + 1 < npairs else None
            ls = slice(hp * LANES, (hp + 1) * LANES)
            m = jnp.max(s, axis=-1, keepdims=True)
            e = jnp.exp2(s - m)
            den = jnp.sum(e, axis=-1, keepdims=True)
            o2 = _dot(e.astype(BF16), v_s[pl.ds(r0, band), ls]) / den
            y_s[pl.ds(r0, pair), ls] = jnp.where(first_head, o2[0:pair], o2[pair:2 * pair]).astype(BF16)
            s = s_next
        return carry

    for p in range(rows // pair):
        body(p, 0)
    o_ref[...] = x + _dot(y_s[...], wout_ref[...])
    k_s[0:hist, :] = k_s[rows:rows + hist, :]
    v_s[0:hist, :] = v_s[rows:rows + hist, :]


def _attn_bias(rel_table):
    heads = rel_table.shape[0]
    hist = LEFT_CHUNKS * CHUNK
    band1 = hist + CHUNK
    ext_idx = jnp.clip(band1 - 1 - jnp.arange(band1 + CHUNK - 1), -(CHUNK - 1), REL_CLIP) + (CHUNK - 1)
    ext = rel_table.astype(F32)[:, ext_idx] * LOG2E
    bias = jnp.stack([ext[:, CHUNK - 1 - t:CHUNK - 1 - t + band1] for t in range(CHUNK)], axis=1)
    neg = jnp.full((heads, CHUNK, CHUNK), NEG_INF, F32)
    both = jnp.concatenate([jnp.concatenate([bias, neg], axis=-1),
                            jnp.concatenate([neg, bias], axis=-1)], axis=1)
    return both.reshape(heads // 2, 4 * CHUNK, band1 + CHUNK)


def _attn_layer(x, gain, w_qkv, rel_table, w_out):
    b, l, d = x.shape
    dh = d // ATTN_HEADS
    hist = LEFT_CHUNKS * CHUNK
    rows = min(ATTN_ROWS, l)
    assert 2 * dh == LANES and rows % (2 * CHUNK) == 0 and rows >= hist
    consts = [gain.reshape(1, d), w_qkv.astype(BF16), _attn_bias(rel_table), w_out.astype(BF16)]
    scratch = [pltpu.VMEM((rows, d), BF16), pltpu.VMEM((hist + rows, d), BF16),
               pltpu.VMEM((hist + rows, d), BF16), pltpu.VMEM((rows, d), BF16)]
    kern = functools.partial(_attn_kernel, rows=rows, hist=hist, d=d, dh=dh)
    return _seq_call(kern, x, consts, scratch, rows, "attn_mixer")


def _gelu_tanh(x):
    return x * jax.nn.sigmoid((2.0 * math.sqrt(2.0 / math.pi)) * (x + 0.044715 * (x * x * x)))


def _lru_kernel(x_ref, gain_ref, win_ref, convw_ref, convb_ref, wa_ref, ba_ref, wx_ref, bx_ref,
                lam_ref, wout_ref, o_ref, tail_s, h_s, a_s, b_s, c_s, *, rows, w, blocks):
    tb = pl.program_id(1)
    tail = tail_s.shape[0]

    @pl.when(tb == 0)
    def _():
        tail_s[...] = jnp.zeros_like(tail_s)
        h_s[...] = jnp.zeros_like(h_s)

    x = x_ref[...]
    h = _rms(x, gain_ref[...]).astype(BF16)
    y = _gelu_tanh(_dot(h, win_ref[:, 0:w]))
    xb = _dot(h, win_ref[:, w:2 * w])

    ext = jnp.concatenate([tail_s[...], xb], axis=0)
    tail_s[...] = xb[rows - tail:rows, :]
    convw = convw_ref[...]
    xc = xb * convw[CONV_WIDTH - 1:CONV_WIDTH, :] + convb_ref[...]
    for i in range(CONV_WIDTH - 1):
        shift = CONV_WIDTH - 1 - i
        xc = xc + pltpu.roll(ext, shift, axis=0)[tail:tail + rows, :] * convw[i:i + 1, :]

    bw = w // blocks
    xcb = xc.astype(BF16)
    ra = jnp.concatenate([_dot(xcb[:, j * bw:(j + 1) * bw], wa_ref[j]) for j in range(blocks)], axis=-1)
    ri = jnp.concatenate([_dot(xcb[:, j * bw:(j + 1) * bw], wx_ref[j]) for j in range(blocks)], axis=-1)
    r = jax.nn.sigmoid(ra + ba_ref[...])
    gi = jax.nn.sigmoid(ri + bx_ref[...])
    lam = lam_ref[...]
    softplus_neg = jnp.maximum(-lam, 0.0) + jnp.log(1.0 + jnp.exp(-jnp.abs(lam)))
    log_a = -LRU_C * r * softplus_neg
    a = jnp.exp(log_a)
    th = jnp.tanh(log_a)
    z = -2.0 * th / (1.0 - th)
    mult = jnp.where(z > 0.0, z * lax.rsqrt(z), 0.0)
    row = lax.broadcasted_iota(jnp.int32, (rows, w), 0)
    mult = jnp.where((row + tb * rows) == 0, 1.0, mult)
    bb = mult * gi * xc

    groups = rows // 8
    a = a.reshape(groups, 8, w)
    bb = bb.reshape(groups, 8, w)
    sub = lax.broadcasted_iota(jnp.int32, (groups, 8, w), 1)
    shift = 1
    while shift < 8:
        keep = sub >= shift
        a_prev = jnp.where(keep, pltpu.roll(a, shift, axis=1), 1.0)
        b_prev = jnp.where(keep, pltpu.roll(bb, shift, axis=1), 0.0)
        bb = a * b_prev + bb
        a = a * a_prev
        shift *= 2
    a_s[...] = a.reshape(rows, w)
    b_s[...] = bb.reshape(rows, w)
    carry = h_s[...]
    for g in range(groups):
        c_s[g:g + 1, :] = carry
        last = 8 * g + 7
        carry = b_s[last:last + 1, :] + a_s[last:last + 1, :] * carry
    h_s[...] = carry
    hs = jnp.concatenate(
        [b_s[8 * g:8 * g + 8, :] + a_s[8 * g:8 * g + 8, :] * c_s[g:g + 1, :] for g in range(groups)], axis=0)
    o_ref[...] = x + _dot((y * hs).astype(BF16), wout_ref[...])


def _lru_layer(x, gain, w_in, conv_w, conv_b, w_a, b_a, w_x, b_x, lam, w_out):
    b, l, d = x.shape
    w = w_out.shape[0]
    blocks = w_a.shape[0]
    rows = min(LRU_ROWS, l)
    row = lambda v: v.reshape(1, -1)
    consts = [row(gain), w_in.astype(BF16), conv_w, row(conv_b), w_a.astype(BF16), row(b_a),
              w_x.astype(BF16), row(b_x), row(lam), w_out.astype(BF16)]
    scratch = [pltpu.VMEM((8, w), F32), pltpu.VMEM((1, w), F32), pltpu.VMEM((rows, w), F32),
               pltpu.VMEM((rows, w), F32), pltpu.VMEM((rows // 8, w), F32)]
    kern = functools.partial(_lru_kernel, rows=rows, w=w, blocks=blocks)
    return _seq_call(kern, x, consts, scratch, rows, "lru_mixer")


def kernel(x, ffn_w_gate_up, ffn_w_down, norm_gains, final_norm_gain, hgrn_w_in, hgrn_lower_bound_logits, hgrn_norm_gain, hgrn_w_out, gla_w_in, gla_w_gate_down, gla_w_gate_up, gla_gate_bias, gla_norm_gain, gla_w_out, attn_w_qkv, attn_rel_bias, attn_w_out, lru_w_in, lru_conv_w, lru_conv_b, lru_w_a, lru_b_a, lru_w_x, lru_b_x, lru_lambda, lru_w_out):
    b, l, d = x.shape
    depth = ffn_w_gate_up.shape[0]
    lower_bounds = jnp.cumsum(jax.nn.softmax(hgrn_lower_bound_logits.astype(F32), axis=0), axis=0)

    def ffn(xx, i, s, final):
        y = _ffn(xx.reshape(b * l, d), norm_gains[i, 2 * s], ffn_w_gate_up[i, s], ffn_w_down[i, s],
                 final_norm_gain, final)
        return y.reshape(b, l, d)

    for i in range(depth):
        m, j = i % N_MIXERS, i // N_MIXERS
        x = ffn(x, i, 0, False)
        g = norm_gains[i, 1]
        if m == 0:
            x = _hgrn_layer(x, g, hgrn_w_in[j], lower_bounds[i], hgrn_norm_gain[j], hgrn_w_out[j])
        elif m == 1:
            x = _gla_layer(x, g, gla_w_in[j], gla_w_gate_down[j], gla_w_gate_up[j], gla_gate_bias[j],
                           gla_norm_gain[j], gla_w_out[j])
        elif m == 2:
            x = _attn_layer(x, g, attn_w_qkv[j], attn_rel_bias[j], attn_w_out[j])
        else:
            x = _lru_layer(x, g, lru_w_in[j], lru_conv_w[j], lru_conv_b[j], lru_w_a[j], lru_b_a[j],
                           lru_w_x[j], lru_b_x[j], lru_lambda[j], lru_w_out[j])
        x = ffn(x, i, 1, i == depth - 1)
    return x
```

```python
import functools
import math

import numpy as np
import jax
import jax.numpy as jnp
from jax import lax
from jax.experimental import pallas as pl
from jax.experimental.pallas import tpu as pltpu

F32 = jnp.float32
BF16 = jnp.bfloat16

EPS = 1e-6
CHUNK = 64
N_MIXERS = 4
NEG_INF = -1e30
HGRN_HEADS = 8
GLA_HEADS = 4
GLA_GATE_NORMALIZER = 16.0
ATTN_HEADS = 16
LEFT_CHUNKS = 8
REL_CLIP = 2 * CHUNK
LRU_HEADS = 4
CONV_WIDTH = 4
LRU_C = 8.0
LOG2E = math.log2(math.e)

LANES = 128
SUBLANES = 8
VMEM_LIMIT_BYTES = 56 * 1024 * 1024

FFN_ROWS = 1024
FFN_COLS = 256
GLA_CHUNK = 64
GLA_ROWS = 1024
ATTN_ROWS = 512
LRU_ROWS = 512


def _const_spec(shape):
    nd = len(shape)
    return pl.BlockSpec(shape, lambda *_: (0,) * nd, pipeline_mode=pl.Buffered(1))


def _rms(x, gain):
    ms = jnp.mean(x * x, axis=-1, keepdims=True)
    return x * lax.rsqrt(ms + EPS) * gain


def _silu(x):
    return x * jax.nn.sigmoid(x)


def _dot(a, b):
    return jnp.dot(a, b, preferred_element_type=F32)


def _dot_nt(a, b):
    return lax.dot_general(a, b, (((1,), (1,)), ((), ())), preferred_element_type=F32)


def _dot_tn(a, b):
    return lax.dot_general(a, b, (((0,), (0,)), ((), ())), preferred_element_type=F32)


def _ffn_kernel(x_ref, gain_ref, wgu_ref, wd_ref, fgain_ref, o_ref, *, d_ff, cols, final_norm):
    x = x_ref[...]
    h = _rms(x, gain_ref[...]).astype(BF16)
    acc = None
    for f in range(d_ff // cols):
        g = _dot(h, wgu_ref[:, f * cols:(f + 1) * cols])
        u = _dot(h, wgu_ref[:, d_ff + f * cols:d_ff + (f + 1) * cols])
        a = (_silu(g) * u).astype(BF16)
        part = _dot(a, wd_ref[f * cols:(f + 1) * cols, :])
        acc = part if acc is None else acc + part
    y = x + 0.5 * acc
    if final_norm:
        y = _rms(y, fgain_ref[...])
    o_ref[...] = y


def _ffn(x2, gain, w_gate_up, w_down, final_gain, final_norm):
    t, d = x2.shape
    d_ff = w_down.shape[0]
    rows = min(FFN_ROWS, t)
    cols = FFN_COLS if d_ff % FFN_COLS == 0 else d_ff
    assert t % rows == 0
    kern = functools.partial(_ffn_kernel, d_ff=d_ff, cols=cols, final_norm=final_norm)
    return pl.pallas_call(
        kern,
        grid=(t // rows,),
        in_specs=[
            pl.BlockSpec((rows, d), lambda i: (i, 0)),
            _const_spec((1, d)),
            _const_spec((d, 2 * d_ff)),
            _const_spec((d_ff, d)),
            _const_spec((1, d)),
        ],
        out_specs=pl.BlockSpec((rows, d), lambda i: (i, 0)),
        out_shape=jax.ShapeDtypeStruct((t, d), F32),
        compiler_params=pltpu.CompilerParams(
            dimension_semantics=("parallel",), vmem_limit_bytes=VMEM_LIMIT_BYTES),
        name="ffn",
    )(x2, gain.reshape(1, d), w_gate_up.astype(BF16), w_down.astype(BF16), final_gain.reshape(1, d))


def _gla_level_count(chunk):
    return int(math.log2(chunk))


def _gla_decay_matrices(chunk):
    t = np.arange(chunk)[:, None]
    j = np.arange(chunk)[None, :]
    mats = [j <= t]
    for lvl in range(_gla_level_count(chunk)):
        half = 1 << lvl
        if half >= SUBLANES:
            break
        ref = (t // (2 * half)) * (2 * half) + half - 1
        mats.append(np.where(t > ref, (j > ref) & (j <= t), (j > t) & (j <= ref)))
    d = np.concatenate(mats, axis=0).astype(np.float32)
    return np.concatenate([d, d, d], axis=-1)


def _split3(g):
    g1 = g.astype(BF16)
    r1 = g - g1.astype(F32)
    g2 = r1.astype(BF16)
    g3 = (r1 - g2.astype(F32)).astype(BF16)
    return jnp.concatenate([g1, g2, g3], axis=0)


def _gla_chunk_loop(q_ref, k_ref, v_ref, g_ref, og_ref, dmat_ref, ngain_ref, st_ref, y_ref,
                    *, rows, heads, dk, dv):
    c = GLA_CHUNK
    levels = _gla_level_count(c)
    small = [lvl for lvl in range(levels) if (1 << lvl) < SUBLANES]
    ti = lax.broadcasted_iota(jnp.int32, (c, c), 0)
    si = lax.broadcasted_iota(jnp.int32, (c, c), 1)
    diag_mask = ti == si
    level_masks = []
    for lvl in range(levels):
        same = (ti >> (lvl + 1)) == (si >> (lvl + 1))
        level_masks.append(same & (((ti >> lvl) & 1) == 1) & (((si >> lvl) & 1) == 0))
    ngain = ngain_ref[...]

    def operands(ci):
        rs = pl.ds(ci * c, c)
        g3 = _split3(g_ref[rs, :] * LOG2E)
        xs = _dot(dmat_ref[...], g3)
        cum = xs[0:c]
        ex = {lvl: jnp.exp2(xs[(i + 1) * c:(i + 2) * c]).astype(BF16) for i, lvl in enumerate(small)}
        for lvl in range(levels):
            if lvl in ex:
                continue
            half = 1 << lvl
            parts = []
            for b0 in range(0, c, 2 * half):
                ref = cum[b0 + half - 1:b0 + half, :]
                parts.append(ref - cum[b0:b0 + half])
                parts.append(cum[b0 + half:b0 + 2 * half] - ref)
            ex[lvl] = jnp.exp2(jnp.concatenate(parts, axis=0)).astype(BF16)
        e_cum = jnp.exp2(cum)
        e_rev = jnp.exp2(cum[c - 1:c, :] - cum).astype(BF16)
        q = q_ref[rs, :]
        k = k_ref[rs, :]
        return dict(q=[q] + [q * ex[lvl] for lvl in range(levels)],
                    k=[k] + [k * ex[lvl] for lvl in range(levels)],
                    q_cum=q * e_cum.astype(BF16), k_rev=k * e_rev, e_last=e_cum[c - 1:c, :])

    def mix(ci, ops):
        rs = pl.ds(ci * c, c)
        a_list = []
        for h in range(heads):
            ks = slice(h * dk, (h + 1) * dk)
            a = jnp.where(diag_mask, _dot_nt(ops["q"][0][:, ks], ops["k"][0][:, ks]), 0.0)
            for lvl in range(levels):
                s = _dot_nt(ops["q"][lvl + 1][:, ks], ops["k"][lvl + 1][:, ks])
                a = jnp.where(level_masks[lvl], s, a)
            a_list.append(a.astype(BF16))
        for h in range(heads):
            ks = slice(h * dk, (h + 1) * dk)
            vs = slice(h * dv, (h + 1) * dv)
            vh = v_ref[rs, vs]
            st = st_ref[h]
            o = _dot(a_list[h], vh) + _dot_nt(ops["q_cum"][:, ks], st.astype(BF16))
            st_ref[h] = st * ops["e_last"][:, ks] + _dot_tn(vh, ops["k_rev"][:, ks])
            o = _rms(o, ngain) * _silu(og_ref[rs, vs])
            y_ref[rs, vs] = o.astype(BF16)

    n = rows // c
    nxt = operands(0)
    for ci in range(n):
        cur = nxt
        if ci + 1 < n:
            nxt = operands(ci + 1)
        mix(ci, cur)


def _hgrn_kernel(x_ref, gain_ref, win_ref, lb_ref, dmat_ref, ngain_ref, wout_ref, o_ref,
                 q_s, k_s, g_s, v_s, og_s, y_s, st_s, *, rows, heads, dk):
    d = heads * dk

    @pl.when(pl.program_id(1) == 0)
    def _():
        st_s[...] = jnp.zeros_like(st_s)

    x = x_ref[...]
    h = _rms(x, gain_ref[...]).astype(BF16)
    q_s[...] = (_silu(_dot(h, win_ref[:, 0:d])) * (dk ** -0.5)).astype(BF16)
    lb = lb_ref[...]
    fgate = lb + (1.0 - lb) * jax.nn.sigmoid(_dot(h, win_ref[:, d:2 * d]))
    k_s[...] = (1.0 - fgate).astype(BF16)
    g_s[...] = jnp.log(fgate)
    v_s[...] = _dot(h, win_ref[:, 2 * d:3 * d]).astype(BF16)
    og_s[...] = _dot(h, win_ref[:, 3 * d:4 * d])
    _gla_chunk_loop(q_s, k_s, v_s, g_s, og_s, dmat_ref, ngain_ref, st_s, y_s,
                    rows=rows, heads=heads, dk=dk, dv=dk)
    o_ref[...] = x + _dot(y_s[...], wout_ref[...])


def _seq_call(kern, x, consts, scratch, rows, name):
    b, l, d = x.shape
    assert l % rows == 0
    xspec = pl.BlockSpec((None, rows, d), lambda bi, ti: (bi, ti, 0))
    return pl.pallas_call(
        kern,
        grid=(b, l // rows),
        in_specs=[xspec] + [_const_spec(c.shape) for c in consts],
        out_specs=xspec,
        out_shape=jax.ShapeDtypeStruct(x.shape, F32),
        scratch_shapes=scratch,
        compiler_params=pltpu.CompilerParams(
            dimension_semantics=("arbitrary", "arbitrary"), vmem_limit_bytes=VMEM_LIMIT_BYTES),
        name=name,
    )(x, *consts)


def _hgrn_layer(x, gain, w_in, lower_bound, norm_gain, w_out):
    b, l, d = x.shape
    heads, dk = HGRN_HEADS, d // HGRN_HEADS
    rows = min(GLA_ROWS, l)
    dmat = jnp.asarray(_gla_decay_matrices(GLA_CHUNK), BF16)
    consts = [gain.reshape(1, d), w_in.astype(BF16), lower_bound.reshape(1, d), dmat,
              norm_gain.reshape(1, dk), w_out.astype(BF16)]
    scratch = [pltpu.VMEM((rows, d), BF16), pltpu.VMEM((rows, d), BF16), pltpu.VMEM((rows, d), F32),
               pltpu.VMEM((rows, d), BF16), pltpu.VMEM((rows, d), F32), pltpu.VMEM((rows, d), BF16),
               pltpu.VMEM((heads, dk, dk), F32)]
    kern = functools.partial(_hgrn_kernel, rows=rows, heads=heads, dk=dk)
    return _seq_call(kern, x, consts, scratch, rows, "hgrn_mixer")


def _log_sigmoid(x):
    return jnp.minimum(x, 0.0) - jnp.log(1.0 + jnp.exp(-jnp.abs(x)))


def _gla_kernel(x_ref, gain_ref, win_ref, wgd_ref, wgu_ref, gbias_ref, dmat_ref, ngain_ref, wout_ref,
                o_ref, q_s, k_s, g_s, v_s, og_s, y_s, st_s, *, rows, heads, dk, dv):
    kd, vd = heads * dk, heads * dv

    @pl.when(pl.program_id(1) == 0)
    def _():
        st_s[...] = jnp.zeros_like(st_s)

    x = x_ref[...]
    h = _rms(x, gain_ref[...]).astype(BF16)
    q_s[...] = (_dot(h, win_ref[:, 0:kd]) * (dk ** -0.5)).astype(BF16)
    k_s[...] = _dot(h, win_ref[:, kd:2 * kd]).astype(BF16)
    v_s[...] = _dot(h, win_ref[:, 2 * kd:2 * kd + vd]).astype(BF16)
    og_s[...] = _dot(h, win_ref[:, 2 * kd + vd:2 * kd + 2 * vd])
    low = _dot(h, wgd_ref[...]).astype(BF16)
    logits = _dot(low, wgu_ref[...]) + gbias_ref[...]
    g_s[...] = _log_sigmoid(logits) * (1.0 / GLA_GATE_NORMALIZER)
    _gla_chunk_loop(q_s, k_s, v_s, g_s, og_s, dmat_ref, ngain_ref, st_s, y_s,
                    rows=rows, heads=heads, dk=dk, dv=dv)
    o_ref[...] = x + _dot(y_s[...], wout_ref[...])


def _gla_layer(x, gain, w_in, w_gate_down, w_gate_up, gate_bias, norm_gain, w_out):
    b, l, d = x.shape
    kd = gate_bias.shape[0]
    vd = w_out.shape[0]
    heads = GLA_HEADS
    dk, dv = kd // heads, vd // heads
    rank = w_gate_down.shape[1]
    rank_pad = -(-rank // LANES) * LANES
    wgd = jnp.pad(w_gate_down, ((0, 0), (0, rank_pad - rank))).astype(BF16)
    wgu = jnp.pad(w_gate_up, ((0, rank_pad - rank), (0, 0))).astype(BF16)
    rows = min(GLA_ROWS, l)
    dmat = jnp.asarray(_gla_decay_matrices(GLA_CHUNK), BF16)
    consts = [gain.reshape(1, d), w_in.astype(BF16), wgd, wgu, gate_bias.reshape(1, kd), dmat,
              norm_gain.reshape(1, dv), w_out.astype(BF16)]
    scratch = [pltpu.VMEM((rows, kd), BF16), pltpu.VMEM((rows, kd), BF16), pltpu.VMEM((rows, kd), F32),
               pltpu.VMEM((rows, vd), BF16), pltpu.VMEM((rows, vd), F32), pltpu.VMEM((rows, vd), BF16),
               pltpu.VMEM((heads, dv, dk), F32)]
    kern = functools.partial(_gla_kernel, rows=rows, heads=heads, dk=dk, dv=dv)
    return _seq_call(kern, x, consts, scratch, rows, "gla_mixer")


def _attn_kernel(x_ref, gain_ref, wqkv_ref, bias_ref, wout_ref, o_ref, q_s, k_s, v_s, y_s,
                 *, rows, hist, d, dh):
    tb = pl.program_id(1)
    pair = 2 * CHUNK
    band = hist + pair
    npairs = d // LANES

    @pl.when(tb == 0)
    def _():
        k_s[0:hist, :] = jnp.zeros((hist, d), BF16)
        v_s[0:hist, :] = jnp.zeros((hist, d), BF16)

    x = x_ref[...]
    h = _rms(x, gain_ref[...]).astype(BF16)
    q_s[...] = (_dot(h, wqkv_ref[:, 0:d]) * (dh ** -0.5 * LOG2E)).astype(BF16)
    k_s[hist:hist + rows, :] = _dot(h, wqkv_ref[:, d:2 * d]).astype(BF16)
    v_s[hist:hist + rows, :] = _dot(h, wqkv_ref[:, 2 * d:3 * d]).astype(BF16)

    lane = lax.broadcasted_iota(jnp.int32, (pair, LANES), 1)
    first_head = lane < dh
    col = lax.broadcasted_iota(jnp.int32, (8, band), 1)

    def body(p, carry):
        r0 = p * pair
        key_mask = jnp.where(col + (tb * rows + r0 - hist) >= 0, 0.0, NEG_INF)
        key_mask = jnp.concatenate([key_mask] * (2 * pair // 8), axis=0)

        def scores(hp):
            ls = slice(hp * LANES, (hp + 1) * LANES)
            q2 = q_s[pl.ds(r0, pair), ls]
            zero = jnp.zeros_like(q2)
            qq = jnp.concatenate([jnp.where(first_head, q2, zero), jnp.where(first_head, zero, q2)], axis=0)
            return _dot_nt(qq, k_s[pl.ds(r0, band), ls]) + bias_ref[hp] + key_mask

        s = scores(0)
        for hp in range(npairs):
            s_next = scores(hp + 1) if hp + 1 < npairs else None
            ls = slice(hp * LANES, (hp + 1) * LANES)
            m = jnp.max(s, axis=-1, keepdims=True)
            e = jnp.exp2(s - m)
            den = jnp.sum(e, axis=-1, keepdims=True)
            o2 = _dot(e.astype(BF16), v_s[pl.ds(r0, band), ls]) / den
            y_s[pl.ds(r0, pair), ls] = jnp.where(first_head, o2[0:pair], o2[pair:2 * pair]).astype(BF16)
            s = s_next
        return carry

    for p in range(rows // pair):
        body(p, 0)
    o_ref[...] = x + _dot(y_s[...], wout_ref[...])
    k_s[0:hist, :] = k_s[rows:rows + hist, :]
    v_s[0:hist, :] = v_s[rows:rows + hist, :]


def _attn_bias(rel_table):
    heads = rel_table.shape[0]
    hist = LEFT_CHUNKS * CHUNK
    band1 = hist + CHUNK
    ext_idx = jnp.clip(band1 - 1 - jnp.arange(band1 + CHUNK - 1), -(CHUNK - 1), REL_CLIP) + (CHUNK - 1)
    ext = rel_table.astype(F32)[:, ext_idx] * LOG2E
    bias = jnp.stack([ext[:, CHUNK - 1 - t:CHUNK - 1 - t + band1] for t in range(CHUNK)], axis=1)
    neg = jnp.full((heads, CHUNK, CHUNK), NEG_INF, F32)
    both = jnp.concatenate([jnp.concatenate([bias, neg], axis=-1),
                            jnp.concatenate([neg, bias], axis=-1)], axis=1)
    return both.reshape(heads // 2, 4 * CHUNK, band1 + CHUNK)


def _attn_layer(x, gain, w_qkv, rel_table, w_out):
    b, l, d = x.shape
    dh = d // ATTN_HEADS
    hist = LEFT_CHUNKS * CHUNK
    rows = min(ATTN_ROWS, l)
    assert 2 * dh == LANES and rows % (2 * CHUNK) == 0 and rows >= hist
    consts = [gain.reshape(1, d), w_qkv.astype(BF16), _attn_bias(rel_table), w_out.astype(BF16)]
    scratch = [pltpu.VMEM((rows, d), BF16), pltpu.VMEM((hist + rows, d), BF16),
               pltpu.VMEM((hist + rows, d), BF16), pltpu.VMEM((rows, d), BF16)]
    kern = functools.partial(_attn_kernel, rows=rows, hist=hist, d=d, dh=dh)
    return _seq_call(kern, x, consts, scratch, rows, "attn_mixer")


def _gelu_tanh(x):
    return x * jax.nn.sigmoid((2.0 * math.sqrt(2.0 / math.pi)) * (x + 0.044715 * (x * x * x)))


def _lru_kernel(x_ref, gain_ref, win_ref, convw_ref, convb_ref, wa_ref, ba_ref, wx_ref, bx_ref,
                lam_ref, wout_ref, o_ref, tail_s, h_s, a_s, b_s, c_s, *, rows, w, blocks):
    tb = pl.program_id(1)
    tail = tail_s.shape[0]

    @pl.when(tb == 0)
    def _():
        tail_s[...] = jnp.zeros_like(tail_s)
        h_s[...] = jnp.zeros_like(h_s)

    x = x_ref[...]
    h = _rms(x, gain_ref[...]).astype(BF16)
    y = _gelu_tanh(_dot(h, win_ref[:, 0:w]))
    xb = _dot(h, win_ref[:, w:2 * w])

    ext = jnp.concatenate([tail_s[...], xb], axis=0)
    tail_s[...] = xb[rows - tail:rows, :]
    convw = convw_ref[...]
    xc = xb * convw[CONV_WIDTH - 1:CONV_WIDTH, :] + convb_ref[...]
    for i in range(CONV_WIDTH - 1):
        shift = CONV_WIDTH - 1 - i
        xc = xc + pltpu.roll(ext, shift, axis=0)[tail:tail + rows, :] * convw[i:i + 1, :]

    bw = w // blocks
    xcb = xc.astype(BF16)
    ra = jnp.concatenate([_dot(xcb[:, j * bw:(j + 1) * bw], wa_ref[j]) for j in range(blocks)], axis=-1)
    ri = jnp.concatenate([_dot(xcb[:, j * bw:(j + 1) * bw], wx_ref[j]) for j in range(blocks)], axis=-1)
    r = jax.nn.sigmoid(ra + ba_ref[...])
    gi = jax.nn.sigmoid(ri + bx_ref[...])
    lam = lam_ref[...]
    softplus_neg = jnp.maximum(-lam, 0.0) + jnp.log(1.0 + jnp.exp(-jnp.abs(lam)))
    log_a = -LRU_C * r * softplus_neg
    a = jnp.exp(log_a)
    th = jnp.tanh(log_a)
    z = -2.0 * th / (1.0 - th)
    mult = jnp.where(z > 0.0, z * lax.rsqrt(z), 0.0)
    row = lax.broadcasted_iota(jnp.int32, (rows, w), 0)
    mult = jnp.where((row + tb * rows) == 0, 1.0, mult)
    bb = mult * gi * xc

    groups = rows // 8
    a = a.reshape(groups, 8, w)
    bb = bb.reshape(groups, 8, w)
    sub = lax.broadcasted_iota(jnp.int32, (groups, 8, w), 1)
    shift = 1
    while shift < 8:
        keep = sub >= shift
        a_prev = jnp.where(keep, pltpu.roll(a, shift, axis=1), 1.0)
        b_prev = jnp.where(keep, pltpu.roll(bb, shift, axis=1), 0.0)
        bb = a * b_prev + bb
        a = a * a_prev
        shift *= 2
    a_s[...] = a.reshape(rows, w)
    b_s[...] = bb.reshape(rows, w)
    carry = h_s[...]
    for g in range(groups):
        c_s[g:g + 1, :] = carry
        last = 8 * g + 7
        carry = b_s[last:last + 1, :] + a_s[last:last + 1, :] * carry
    h_s[...] = carry
    hs = jnp.concatenate(
        [b_s[8 * g:8 * g + 8, :] + a_s[8 * g:8 * g + 8, :] * c_s[g:g + 1, :] for g in range(groups)], axis=0)
    o_ref[...] = x + _dot((y * hs).astype(BF16), wout_ref[...])


def _lru_layer(x, gain, w_in, conv_w, conv_b, w_a, b_a, w_x, b_x, lam, w_out):
    b, l, d = x.shape
    w = w_out.shape[0]
    blocks = w_a.shape[0]
    rows = min(LRU_ROWS, l)
    row = lambda v: v.reshape(1, -1)
    consts = [row(gain), w_in.astype(BF16), conv_w, row(conv_b), w_a.astype(BF16), row(b_a),
              w_x.astype(BF16), row(b_x), row(lam), w_out.astype(BF16)]
    scratch = [pltpu.VMEM((8, w), F32), pltpu.VMEM((1, w), F32), pltpu.VMEM((rows, w), F32),
               pltpu.VMEM((rows, w), F32), pltpu.VMEM((rows // 8, w), F32)]
    kern = functools.partial(_lru_kernel, rows=rows, w=w, blocks=blocks)
    return _seq_call(kern, x, consts, scratch, rows, "lru_mixer")


def kernel(x, ffn_w_gate_up, ffn_w_down, norm_gains, final_norm_gain, hgrn_w_in, hgrn_lower_bound_logits, hgrn_norm_gain, hgrn_w_out, gla_w_in, gla_w_gate_down, gla_w_gate_up, gla_gate_bias, gla_norm_gain, gla_w_out, attn_w_qkv, attn_rel_bias, attn_w_out, lru_w_in, lru_conv_w, lru_conv_b, lru_w_a, lru_b_a, lru_w_x, lru_b_x, lru_lambda, lru_w_out):
    b, l, d = x.shape
    depth = ffn_w_gate_up.shape[0]
    lower_bounds = jnp.cumsum(jax.nn.softmax(hgrn_lower_bound_logits.astype(F32), axis=0), axis=0)

    def ffn(xx, i, s, final):
        y = _ffn(xx.reshape(b * l, d), norm_gains[i, 2 * s], ffn_w_gate_up[i, s], ffn_w_down[i, s],
                 final_norm_gain, final)
        return y.reshape(b, l, d)

    for i in range(depth):
        m, j = i % N_MIXERS, i // N_MIXERS
        x = ffn(x, i, 0, False)
        g = norm_gains[i, 1]
        if m == 0:
            x = _hgrn_layer(x, g, hgrn_w_in[j], lower_bounds[i], hgrn_norm_gain[j], hgrn_w_out[j])
        elif m == 1:
            x = _gla_layer(x, g, gla_w_in[j], gla_w_gate_down[j], gla_w_gate_up[j], gla_gate_bias[j],
                           gla_norm_gain[j], gla_w_out[j])
        elif m == 2:
            x = _attn_layer(x, g, attn_w_qkv[j], attn_rel_bias[j], attn_w_out[j])
        else:
            x = _lru_layer(x, g, lru_w_in[j], lru_conv_w[j], lru_conv_b[j], lru_w_a[j], lru_b_a[j],
                           lru_w_x[j], lru_b_x[j], lru_lambda[j], lru_w_out[j])
        x = ffn(x, i, 1, i == depth - 1)
    return x
```

```python
import functools
import math

import numpy as np
import jax
import jax.numpy as jnp
from jax import lax
from jax.experimental import pallas as pl
from jax.experimental.pallas import tpu as pltpu

F32 = jnp.float32
BF16 = jnp.bfloat16

EPS = 1e-6
CHUNK = 64
N_MIXERS = 4
NEG_INF = -1e30
HGRN_HEADS = 8
GLA_HEADS = 4
GLA_GATE_NORMALIZER = 16.0
ATTN_HEADS = 16
LEFT_CHUNKS = 8
REL_CLIP = 2 * CHUNK
LRU_HEADS = 4
CONV_WIDTH = 4
LRU_C = 8.0
LOG2E = math.log2(math.e)

LANES = 128
SUBLANES = 8
VMEM_LIMIT_BYTES = 56 * 1024 * 1024

FFN_ROWS = 1024
FFN_COLS = 256
GLA_CHUNK = 64
GLA_ROWS = 1024
ATTN_ROWS = 512
LRU_ROWS = 512


def _const_spec(shape):
    nd = len(shape)
    return pl.BlockSpec(shape, lambda *_: (0,) * nd, pipeline_mode=pl.Buffered(1))


def _rms(x, gain):
    ms = jnp.mean(x * x, axis=-1, keepdims=True)
    return x * lax.rsqrt(ms + EPS) * gain


def _silu(x):
    return x * jax.nn.sigmoid(x)


def _dot(a, b):
    return jnp.dot(a, b, preferred_element_type=F32)


def _dot_nt(a, b):
    return lax.dot_general(a, b, (((1,), (1,)), ((), ())), preferred_element_type=F32)


def _dot_tn(a, b):
    return lax.dot_general(a, b, (((0,), (0,)), ((), ())), preferred_element_type=F32)


def _ffn_kernel(x_ref, gain_ref, wgu_ref, wd_ref, fgain_ref, o_ref, *, d_ff, cols, final_norm):
    x = x_ref[...]
    h = _rms(x, gain_ref[...]).astype(BF16)
    acc = None
    for f in range(d_ff // cols):
        g = _dot(h, wgu_ref[:, f * cols:(f + 1) * cols])
        u = _dot(h, wgu_ref[:, d_ff + f * cols:d_ff + (f + 1) * cols])
        a = (_silu(g) * u).astype(BF16)
        part = _dot(a, wd_ref[f * cols:(f + 1) * cols, :])
        acc = part if acc is None else acc + part
    y = x + 0.5 * acc
    if final_norm:
        y = _rms(y, fgain_ref[...])
    o_ref[...] = y


def _ffn(x2, gain, w_gate_up, w_down, final_gain, final_norm):
    t, d = x2.shape
    d_ff = w_down.shape[0]
    rows = min(FFN_ROWS, t)
    cols = FFN_COLS if d_ff % FFN_COLS == 0 else d_ff
    assert t % rows == 0
    kern = functools.partial(_ffn_kernel, d_ff=d_ff, cols=cols, final_norm=final_norm)
    return pl.pallas_call(
        kern,
        grid=(t // rows,),
        in_specs=[
            pl.BlockSpec((rows, d), lambda i: (i, 0)),
            _const_spec((1, d)),
            _const_spec((d, 2 * d_ff)),
            _const_spec((d_ff, d)),
            _const_spec((1, d)),
        ],
        out_specs=pl.BlockSpec((rows, d), lambda i: (i, 0)),
        out_shape=jax.ShapeDtypeStruct((t, d), F32),
        compiler_params=pltpu.CompilerParams(
            dimension_semantics=("parallel",), vmem_limit_bytes=VMEM_LIMIT_BYTES),
        name="ffn",
    )(x2, gain.reshape(1, d), w_gate_up.astype(BF16), w_down.astype(BF16), final_gain.reshape(1, d))


def _gla_level_count(chunk):
    return int(math.log2(chunk))


def _gla_decay_matrices(chunk):
    t = np.arange(chunk)[:, None]
    j = np.arange(chunk)[None, :]
    mats = [j <= t]
    for lvl in range(_gla_level_count(chunk)):
        half = 1 << lvl
        if half >= SUBLANES:
            break
        ref = (t // (2 * half)) * (2 * half) + half - 1
        mats.append(np.where(t > ref, (j > ref) & (j <= t), (j > t) & (j <= ref)))
    d = np.concatenate(mats, axis=0).astype(np.float32)
    return np.concatenate([d, d, d], axis=-1)


def _split3(g):
    g1 = g.astype(BF16)
    r1 = g - g1.astype(F32)
    g2 = r1.astype(BF16)
    g3 = (r1 - g2.astype(F32)).astype(BF16)
    return jnp.concatenate([g1, g2, g3], axis=0)


def _gla_chunk_loop(q_ref, k_ref, v_ref, g_ref, og_ref, dmat_ref, ngain_ref, st_ref, y_ref,
                    *, rows, heads, dk, dv):
    c = GLA_CHUNK
    levels = _gla_level_count(c)
    small = [lvl for lvl in range(levels) if (1 << lvl) < SUBLANES]
    ti = lax.broadcasted_iota(jnp.int32, (c, c), 0)
    si = lax.broadcasted_iota(jnp.int32, (c, c), 1)
    diag_mask = ti == si
    level_masks = []
    for lvl in range(levels):
        same = (ti >> (lvl + 1)) == (si >> (lvl + 1))
        level_masks.append(same & (((ti >> lvl) & 1) == 1) & (((si >> lvl) & 1) == 0))
    ngain = ngain_ref[...]

    def operands(ci):
        rs = pl.ds(ci * c, c)
        g3 = _split3(g_ref[rs, :] * LOG2E)
        xs = _dot(dmat_ref[...], g3)
        cum = xs[0:c]
        ex = {lvl: jnp.exp2(xs[(i + 1) * c:(i + 2) * c]).astype(BF16) for i, lvl in enumerate(small)}
        for lvl in range(levels):
            if lvl in ex:
                continue
            half = 1 << lvl
            parts = []
            for b0 in range(0, c, 2 * half):
                ref = cum[b0 + half - 1:b0 + half, :]
                parts.append(ref - cum[b0:b0 + half])
                parts.append(cum[b0 + half:b0 + 2 * half] - ref)
            ex[lvl] = jnp.exp2(jnp.concatenate(parts, axis=0)).astype(BF16)
        e_cum = jnp.exp2(cum)
        e_rev = jnp.exp2(cum[c - 1:c, :] - cum).astype(BF16)
        q = q_ref[rs, :]
        k = k_ref[rs, :]
        return dict(q=[q] + [q * ex[lvl] for lvl in range(levels)],
                    k=[k] + [k * ex[lvl] for lvl in range(levels)],
                    q_cum=q * e_cum.astype(BF16), k_rev=k * e_rev, e_last=e_cum[c - 1:c, :])

    def mix(ci, ops):
        rs = pl.ds(ci * c, c)
        a_list = []
        for h in range(heads):
            ks = slice(h * dk, (h + 1) * dk)
            a = jnp.where(diag_mask, _dot_nt(ops["q"][0][:, ks], ops["k"][0][:, ks]), 0.0)
            for lvl in range(levels):
                s = _dot_nt(ops["q"][lvl + 1][:, ks], ops["k"][lvl + 1][:, ks])
                a = jnp.where(level_masks[lvl], s, a)
            a_list.append(a.astype(BF16))
        for h in range(heads):
            ks = slice(h * dk, (h + 1) * dk)
            vs = slice(h * dv, (h + 1) * dv)
            vh = v_ref[rs, vs]
            st = st_ref[h]
            o = _dot(a_list[h], vh) + _dot_nt(ops["q_cum"][:, ks], st.astype(BF16))
            st_ref[h] = st * ops["e_last"][:, ks] + _dot_tn(vh, ops["k_rev"][:, ks])
            o = _rms(o, ngain) * _silu(og_ref[rs, vs])
            y_ref[rs, vs] = o.astype(BF16)

    n = rows // c
    nxt = operands(0)
    for ci in range(n):
        cur = nxt
        if ci + 1 < n:
            nxt = operands(ci + 1)
        mix(ci, cur)


def _hgrn_kernel(x_ref, gain_ref, win_ref, lb_ref, dmat_ref, ngain_ref, wout_ref, o_ref,
                 q_s, k_s, g_s, v_s, og_s, y_s, st_s, *, rows, heads, dk):
    d = heads * dk

    @pl.when(pl.program_id(1) == 0)
    def _():
        st_s[...] = jnp.zeros_like(st_s)

    x = x_ref[...]
    h = _rms(x, gain_ref[...]).astype(BF16)
    q_s[...] = (_silu(_dot(h, win_ref[:, 0:d])) * (dk ** -0.5)).astype(BF16)
    lb = lb_ref[...]
    fgate = lb + (1.0 - lb) * jax.nn.sigmoid(_dot(h, win_ref[:, d:2 * d]))
    k_s[...] = (1.0 - fgate).astype(BF16)
    g_s[...] = jnp.log(fgate)
    v_s[...] = _dot(h, win_ref[:, 2 * d:3 * d]).astype(BF16)
    og_s[...] = _dot(h, win_ref[:, 3 * d:4 * d])
    _gla_chunk_loop(q_s, k_s, v_s, g_s, og_s, dmat_ref, ngain_ref, st_s, y_s,
                    rows=rows, heads=heads, dk=dk, dv=dk)
    o_ref[...] = x + _dot(y_s[...], wout_ref[...])


def _seq_call(kern, x, consts, scratch, rows, name):
    b, l, d = x.shape
    assert l % rows == 0
    xspec = pl.BlockSpec((None, rows, d), lambda bi, ti: (bi, ti, 0))
    return pl.pallas_call(
        kern,
        grid=(b, l // rows),
        in_specs=[xspec] + [_const_spec(c.shape) for c in consts],
        out_specs=xspec,
        out_shape=jax.ShapeDtypeStruct(x.shape, F32),
        scratch_shapes=scratch,
        compiler_params=pltpu.CompilerParams(
            dimension_semantics=("arbitrary", "arbitrary"), vmem_limit_bytes=VMEM_LIMIT_BYTES),
        name=name,
    )(x, *consts)


def _hgrn_layer(x, gain, w_in, lower_bound, norm_gain, w_out):
    b, l, d = x.shape
    heads, dk = HGRN_HEADS, d // HGRN_HEADS
    rows = min(GLA_ROWS, l)
    dmat = jnp.asarray(_gla_decay_matrices(GLA_CHUNK), BF16)
    consts = [gain.reshape(1, d), w_in.astype(BF16), lower_bound.reshape(1, d), dmat,
              norm_gain.reshape(1, dk), w_out.astype(BF16)]
    scratch = [pltpu.VMEM((rows, d), BF16), pltpu.VMEM((rows, d), BF16), pltpu.VMEM((rows, d), F32),
               pltpu.VMEM((rows, d), BF16), pltpu.VMEM((rows, d), F32), pltpu.VMEM((rows, d), BF16),
               pltpu.VMEM((heads, dk, dk), F32)]
    kern = functools.partial(_hgrn_kernel, rows=rows, heads=heads, dk=dk)
    return _seq_call(kern, x, consts, scratch, rows, "hgrn_mixer")


def _log_sigmoid(x):
    return jnp.minimum(x, 0.0) - jnp.log(1.0 + jnp.exp(-jnp.abs(x)))


def _gla_kernel(x_ref, gain_ref, win_ref, wgd_ref, wgu_ref, gbias_ref, dmat_ref, ngain_ref, wout_ref,
                o_ref, q_s, k_s, g_s, v_s, og_s, y_s, st_s, *, rows, heads, dk, dv):
    kd, vd = heads * dk, heads * dv

    @pl.when(pl.program_id(1) == 0)
    def _():
        st_s[...] = jnp.zeros_like(st_s)

    x = x_ref[...]
    h = _rms(x, gain_ref[...]).astype(BF16)
    q_s[...] = (_dot(h, win_ref[:, 0:kd]) * (dk ** -0.5)).astype(BF16)
    k_s[...] = _dot(h, win_ref[:, kd:2 * kd]).astype(BF16)
    v_s[...] = _dot(h, win_ref[:, 2 * kd:2 * kd + vd]).astype(BF16)
    og_s[...] = _dot(h, win_ref[:, 2 * kd + vd:2 * kd + 2 * vd])
    low = _dot(h, wgd_ref[...]).astype(BF16)
    logits = _dot(low, wgu_ref[...]) + gbias_ref[...]
    g_s[...] = _log_sigmoid(logits) * (1.0 / GLA_GATE_NORMALIZER)
    _gla_chunk_loop(q_s, k_s, v_s, g_s, og_s, dmat_ref, ngain_ref, st_s, y_s,
                    rows=rows, heads=heads, dk=dk, dv=dv)
    o_ref[...] = x + _dot(y_s[...], wout_ref[...])


def _gla_layer(x, gain, w_in, w_gate_down, w_gate_up, gate_bias, norm_gain, w_out):
    b, l, d = x.shape
    kd = gate_bias.shape[0]
    vd = w_out.shape[0]
    heads = GLA_HEADS
    dk, dv = kd // heads, vd // heads
    rank = w_gate_down.shape[1]
    rank_pad = -(-rank // LANES) * LANES
    wgd = jnp.pad(w_gate_down, ((0, 0), (0, rank_pad - rank))).astype(BF16)
    wgu = jnp.pad(w_gate_up, ((0, rank_pad - rank), (0, 0))).astype(BF16)
    rows = min(GLA_ROWS, l)
    dmat = jnp.asarray(_gla_decay_matrices(GLA_CHUNK), BF16)
    consts = [gain.reshape(1, d), w_in.astype(BF16), wgd, wgu, gate_bias.reshape(1, kd), dmat,
              norm_gain.reshape(1, dv), w_out.astype(BF16)]
    scratch = [pltpu.VMEM((rows, kd), BF16), pltpu.VMEM((rows, kd), BF16), pltpu.VMEM((rows, kd), F32),
               pltpu.VMEM((rows, vd), BF16), pltpu.VMEM((rows, vd), F32), pltpu.VMEM((rows, vd), BF16),
               pltpu.VMEM((heads, dv, dk), F32)]
    kern = functools.partial(_gla_kernel, rows=rows, heads=heads, dk=dk, dv=dv)
    return _seq_call(kern, x, consts, scratch, rows, "gla_mixer")


def _attn_kernel(x_ref, gain_ref, wqkv_ref, bias_ref, wout_ref, o_ref, q_s, k_s, v_s, y_s,
                 *, rows, hist, d, dh):
    tb = pl.program_id(1)
    pair = 2 * CHUNK
    band = hist + pair
    npairs = d // LANES

    @pl.when(tb == 0)
    def _():
        k_s[0:hist, :] = jnp.zeros((hist, d), BF16)
        v_s[0:hist, :] = jnp.zeros((hist, d), BF16)

    x = x_ref[...]
    h = _rms(x, gain_ref[...]).astype(BF16)
    q_s[...] = (_dot(h, wqkv_ref[:, 0:d]) * (dh ** -0.5 * LOG2E)).astype(BF16)
    k_s[hist:hist + rows, :] = _dot(h, wqkv_ref[:, d:2 * d]).astype(BF16)
    v_s[hist:hist + rows, :] = _dot(h, wqkv_ref[:, 2 * d:3 * d]).astype(BF16)

    lane = lax.broadcasted_iota(jnp.int32, (pair, LANES), 1)
    first_head = lane < dh
    col = lax.broadcasted_iota(jnp.int32, (SUBLANES, band), 1)

    def body(p, carry):
        r0 = p * pair
        key_mask = jnp.where(col + (tb * rows + r0 - hist) >= 0, 0.0, NEG_INF)
        key_mask = jnp.concatenate([key_mask] * (2 * pair // SUBLANES), axis=0)

        def scores(hp):
            ls = slice(hp * LANES, (hp + 1) * LANES)
            q2 = q_s[pl.ds(r0, pair), ls]
            zero = jnp.zeros_like(q2)
            qq = jnp.concatenate([jnp.where(first_head, q2, zero), jnp.where(first_head, zero, q2)], axis=0)
            return _dot_nt(qq, k_s[pl.ds(r0, band), ls]) + bias_ref[hp] + key_mask

        pending = [scores(0), scores(1)]
        for hp in range(npairs):
            if hp + 2 < npairs:
                pending.append(scores(hp + 2))
            s = pending.pop(0)
            ls = slice(hp * LANES, (hp + 1) * LANES)
            m = jnp.max(s, axis=-1, keepdims=True)
            e = jnp.exp2(s - m)
            den = jnp.sum(e, axis=-1, keepdims=True)
            o2 = _dot(e.astype(BF16), v_s[pl.ds(r0, band), ls]) / den
            y_s[pl.ds(r0, pair), ls] = jnp.where(first_head, o2[0:pair], o2[pair:2 * pair]).astype(BF16)
        return carry

    for p in range(rows // pair):
        body(p, 0)
    o_ref[...] = x + _dot(y_s[...], wout_ref[...])
    k_s[0:hist, :] = k_s[rows:rows + hist, :]
    v_s[0:hist, :] = v_s[rows:rows + hist, :]


def _attn_bias(rel_table):
    heads = rel_table.shape[0]
    hist = LEFT_CHUNKS * CHUNK
    band1 = hist + CHUNK
    ext_idx = jnp.clip(band1 - 1 - jnp.arange(band1 + CHUNK - 1), -(CHUNK - 1), REL_CLIP) + (CHUNK - 1)
    ext = rel_table.astype(F32)[:, ext_idx] * LOG2E
    bias = jnp.stack([ext[:, CHUNK - 1 - t:CHUNK - 1 - t + band1] for t in range(CHUNK)], axis=1)
    neg = jnp.full((heads, CHUNK, CHUNK), NEG_INF, F32)
    both = jnp.concatenate([jnp.concatenate([bias, neg], axis=-1),
                            jnp.concatenate([neg, bias], axis=-1)], axis=1)
    return both.reshape(heads // 2, 4 * CHUNK, band1 + CHUNK)


def _attn_layer(x, gain, w_qkv, rel_table, w_out):
    b, l, d = x.shape
    dh = d // ATTN_HEADS
    hist = LEFT_CHUNKS * CHUNK
    rows = min(ATTN_ROWS, l)
    assert 2 * dh == LANES and rows % (2 * CHUNK) == 0 and rows >= hist
    consts = [gain.reshape(1, d), w_qkv.astype(BF16), _attn_bias(rel_table), w_out.astype(BF16)]
    scratch = [pltpu.VMEM((rows, d), BF16), pltpu.VMEM((hist + rows, d), BF16),
               pltpu.VMEM((hist + rows, d), BF16), pltpu.VMEM((rows, d), BF16)]
    kern = functools.partial(_attn_kernel, rows=rows, hist=hist, d=d, dh=dh)
    return _seq_call(kern, x, consts, scratch, rows, "attn_mixer")


def _gelu_tanh(x):
    return x * jax.nn.sigmoid((2.0 * math.sqrt(2.0 / math.pi)) * (x + 0.044715 * (x * x * x)))


def _lru_kernel(x_ref, gain_ref, win_ref, convw_ref, convb_ref, wa_ref, ba_ref, wx_ref, bx_ref,
                lam_ref, wout_ref, o_ref, tail_s, h_s, a_s, b_s, c_s, *, rows, w, blocks):
    tb = pl.program_id(1)
    tail = tail_s.shape[0]

    @pl.when(tb == 0)
    def _():
        tail_s[...] = jnp.zeros_like(tail_s)
        h_s[...] = jnp.zeros_like(h_s)

    x = x_ref[...]
    h = _rms(x, gain_ref[...]).astype(BF16)
    y = _gelu_tanh(_dot(h, win_ref[:, 0:w]))
    xb = _dot(h, win_ref[:, w:2 * w])

    groups = rows // SUBLANES
    ext = jnp.concatenate([tail_s[...], xb], axis=0).reshape(groups + 1, SUBLANES, w)
    tail_s[...] = xb[rows - tail:rows, :]
    convw = convw_ref[...]
    sub = lax.broadcasted_iota(jnp.int32, (groups, SUBLANES, w), 1)
    xc = xb * convw[CONV_WIDTH - 1:CONV_WIDTH, :] + convb_ref[...]
    for i in range(CONV_WIDTH - 1):
        shift = CONV_WIDTH - 1 - i
        rot = pltpu.roll(ext, shift, axis=1)
        shifted = jnp.where(sub >= shift, rot[1:], rot[:-1])
        xc = xc + shifted.reshape(rows, w) * convw[i:i + 1, :]

    bw = w // blocks
    xcb = xc.astype(BF16)
    ra = jnp.concatenate([_dot(xcb[:, j * bw:(j + 1) * bw], wa_ref[j]) for j in range(blocks)], axis=-1)
    ri = jnp.concatenate([_dot(xcb[:, j * bw:(j + 1) * bw], wx_ref[j]) for j in range(blocks)], axis=-1)
    r = jax.nn.sigmoid(ra + ba_ref[...])
    gi = jax.nn.sigmoid(ri + bx_ref[...])
    lam = lam_ref[...]
    softplus_neg = jnp.maximum(-lam, 0.0) + jnp.log(1.0 + jnp.exp(-jnp.abs(lam)))
    log_a = r * (-LRU_C * softplus_neg)
    a = jnp.exp(log_a)
    th = jnp.tanh(log_a)
    z = -2.0 * th / (1.0 - th)
    mult = jnp.where(z > 0.0, z * lax.rsqrt(z), 0.0)
    row = lax.broadcasted_iota(jnp.int32, (SUBLANES, w), 0)
    head = jnp.where((row + tb * rows) == 0, 1.0, mult[0:SUBLANES])
    mult = jnp.concatenate([head, mult[SUBLANES:]], axis=0)
    bb = mult * gi * xc

    a = a.reshape(groups, SUBLANES, w)
    bb = bb.reshape(groups, SUBLANES, w)
    shift = 1
    while shift < SUBLANES:
        keep = sub >= shift
        a_prev = jnp.where(keep, pltpu.roll(a, shift, axis=1), 1.0)
        b_prev = jnp.where(keep, pltpu.roll(bb, shift, axis=1), 0.0)
        bb = a * b_prev + bb
        a = a * a_prev
        shift *= 2
    a_s[...] = a.reshape(rows, w)
    b_s[...] = bb.reshape(rows, w)
    carry = h_s[...]
    for g in range(groups):
        c_s[g:g + 1, :] = carry
        last = SUBLANES * g + SUBLANES - 1
        carry = b_s[last:last + 1, :] + a_s[last:last + 1, :] * carry
    h_s[...] = carry
    hs = jnp.concatenate(
        [b_s[SUBLANES * g:SUBLANES * (g + 1), :] + a_s[SUBLANES * g:SUBLANES * (g + 1), :] * c_s[g:g + 1, :]
         for g in range(groups)], axis=0)
    o_ref[...] = x + _dot((y * hs).astype(BF16), wout_ref[...])


def _lru_layer(x, gain, w_in, conv_w, conv_b, w_a, b_a, w_x, b_x, lam, w_out):
    b, l, d = x.shape
    w = w_out.shape[0]
    blocks = w_a.shape[0]
    rows = min(LRU_ROWS, l)
    row = lambda v: v.reshape(1, -1)
    consts = [row(gain), w_in.astype(BF16), conv_w, row(conv_b), w_a.astype(BF16), row(b_a),
              w_x.astype(BF16), row(b_x), row(lam), w_out.astype(BF16)]
    scratch = [pltpu.VMEM((SUBLANES, w), F32), pltpu.VMEM((1, w), F32), pltpu.VMEM((rows, w), F32),
               pltpu.VMEM((rows, w), F32), pltpu.VMEM((rows // SUBLANES, w), F32)]
    kern = functools.partial(_lru_kernel, rows=rows, w=w, blocks=blocks)
    return _seq_call(kern, x, consts, scratch, rows, "lru_mixer")


def kernel(x, ffn_w_gate_up, ffn_w_down, norm_gains, final_norm_gain, hgrn_w_in, hgrn_lower_bound_logits, hgrn_norm_gain, hgrn_w_out, gla_w_in, gla_w_gate_down, gla_w_gate_up, gla_gate_bias, gla_norm_gain, gla_w_out, attn_w_qkv, attn_rel_bias, attn_w_out, lru_w_in, lru_conv_w, lru_conv_b, lru_w_a, lru_b_a, lru_w_x, lru_b_x, lru_lambda, lru_w_out):
    b, l, d = x.shape
    depth = ffn_w_gate_up.shape[0]
    lower_bounds = jnp.cumsum(jax.nn.softmax(hgrn_lower_bound_logits.astype(F32), axis=0), axis=0)

    def ffn(xx, i, s, final):
        y = _ffn(xx.reshape(b * l, d), norm_gains[i, 2 * s], ffn_w_gate_up[i, s], ffn_w_down[i, s],
                 final_norm_gain, final)
        return y.reshape(b, l, d)

    for i in range(depth):
        m, j = i % N_MIXERS, i // N_MIXERS
        x = ffn(x, i, 0, False)
        g = norm_gains[i, 1]
        if m == 0:
            x = _hgrn_layer(x, g, hgrn_w_in[j], lower_bounds[i], hgrn_norm_gain[j], hgrn_w_out[j])
        elif m == 1:
            x = _gla_layer(x, g, gla_w_in[j], gla_w_gate_down[j], gla_w_gate_up[j], gla_gate_bias[j],
                           gla_norm_gain[j], gla_w_out[j])
        elif m == 2:
            x = _attn_layer(x, g, attn_w_qkv[j], attn_rel_bias[j], attn_w_out[j])
        else:
            x = _lru_layer(x, g, lru_w_in[j], lru_conv_w[j], lru_conv_b[j], lru_w_a[j], lru_b_a[j],
                           lru_w_x[j], lru_b_x[j], lru_lambda[j], lru_w_out[j])
        x = ffn(x, i, 1, i == depth - 1)
    return x
```
